```python
import math
import jax, jax.numpy as jnp
from jax import lax
import numpy as np

D_MODEL = 1024
BATCH = 4
SEQ = 8192
DEPTH = 1

RET_WIDTH = D_MODEL
RET_HEADS = 8
RET_HEAD_DIM = RET_WIDTH // RET_HEADS
RET_CHUNK = 128
ROPE_BASE = 10000.0
LRU_WIDTH = D_MODEL
LRU_BLOCKS = 8
LRU_BLOCK_DIM = LRU_WIDTH // LRU_BLOCKS
LRU_C = 8.0
CONV_WIDTH = 4
MIX_WIDTH = RET_WIDTH + LRU_WIDTH
IN_SPLITS = (RET_WIDTH, 2 * RET_WIDTH, 3 * RET_WIDTH, 4 * RET_WIDTH, 4 * RET_WIDTH + LRU_WIDTH)
IN_WIDTH = 4 * RET_WIDTH + 2 * LRU_WIDTH
NORM_EPS = 1e-6

kernel_name = "hybrid_retention_rglru_parallel_heads"


def rms_norm(x, g):
    x32 = x.astype(jnp.float32)
    y = x32 * lax.rsqrt(jnp.mean(x32 * x32, axis=-1, keepdims=True) + NORM_EPS)
    return (y * g.astype(jnp.float32)).astype(x.dtype)


def rotary(x):
    T, d = x.shape[1], x.shape[-1]
    inv_freq = ROPE_BASE ** (-jnp.arange(0, d, 2, dtype=jnp.float32) / d)
    ang = jnp.arange(T, dtype=jnp.float32)[:, None] * inv_freq[None, :]
    cos = jnp.cos(ang)[None, :, None, :]
    sin = jnp.sin(ang)[None, :, None, :]
    x1, x2 = x[..., : d // 2], x[..., d // 2:]
    return jnp.concatenate([x1 * cos - x2 * sin, x1 * sin + x2 * cos], axis=-1)


def chunkwise_retention(q, k, v):
    B, T, H, d = q.shape
    C = RET_CHUNK
    nc = T // C
    log_g = jnp.log1p(-jnp.exp2(-5.0 - jnp.arange(H, dtype=jnp.float32)))
    idx = jnp.arange(C, dtype=jnp.float32)
    rel = idx[:, None] - idx[None, :]
    decay_mat = jnp.where(rel[None] >= 0,
                          jnp.exp(jnp.maximum(rel, 0.0)[None] * log_g[:, None, None]), 0.0)
    zeta = jnp.exp((C - 1 - idx)[None, :] * log_g[:, None])
    xi = jnp.exp((idx + 1)[None, :] * log_g[:, None])
    chunk_decay = jnp.exp(C * log_g)

    qc = q.reshape(B, nc, C, H, d)
    kc = k.reshape(B, nc, C, H, d)
    vc = v.reshape(B, nc, C, H, d)

    scores = jnp.einsum('bnqhd,bnkhd->bnhqk', qc, kc) * decay_mat[None, None]
    intra = jnp.einsum('bnhqk,bnkhe->bnqhe', scores, vc)

    kv = jnp.einsum('bnkhd,hk,bnkhe->nbhde', kc, zeta, vc)

    def step(state, kv_n):
        new_state = chunk_decay[None, :, None, None] * state + kv_n
        return new_state, state

    _, prev_states = lax.scan(step, jnp.zeros((B, H, d, d), jnp.float32), kv)
    prev_states = jnp.moveaxis(prev_states, 0, 1)
    cross = jnp.einsum('bnqhd,bnhde->bnqhe', qc, prev_states) * xi.T[None, None, :, :, None]
    return (intra + cross).reshape(B, T, H, d)


def group_norm_heads(o):
    mu = jnp.mean(o, axis=-1, keepdims=True)
    var = jnp.mean(jnp.square(o - mu), axis=-1, keepdims=True)
    return (o - mu) * lax.rsqrt(var + NORM_EPS)


def causal_depthwise_conv(x, w, b):
    T = x.shape[1]
    xp = jnp.pad(x, ((0, 0), (CONV_WIDTH - 1, 0), (0, 0)))
    out = b[None, None, :]
    for j in range(CONV_WIDTH):
        out = out + xp[:, j:j + T, :] * w[j][None, None, :]
    return out


def block_diag_linear(x, w, b):
    B, T, _ = x.shape
    xb = x.reshape(B, T, LRU_BLOCKS, LRU_BLOCK_DIM)
    y = jnp.einsum('btni,nio->btno', xb, w) + b[None, None]
    return y.reshape(B, T, LRU_WIDTH)


def rg_lru(x, gate_a_w, gate_a_b, gate_x_w, gate_x_b, lru_lambda):
    r = jax.nn.sigmoid(block_diag_linear(x, gate_a_w, gate_a_b))
    i = jax.nn.sigmoid(block_diag_linear(x, gate_x_w, gate_x_b))
    log_a = -LRU_C * r * jax.nn.softplus(-lru_lambda)[None, None, :]
    a = jnp.exp(log_a)
    mult = jnp.sqrt(jnp.maximum(1.0 - jnp.exp(2.0 * log_a), 0.0))
    b = mult * (i * x)

    def combine(left, right):
        a_l, b_l = left
        a_r, b_r = right
        return a_l * a_r, a_r * b_l + b_r

    _, h = lax.associative_scan(combine, (a, b), axis=1)
    return h


def setup_inputs(seed: int = 0) -> dict:
    key = jax.random.key(seed)
    ks = jax.random.split(key, 14)
    f32 = jnp.float32
    x = jax.random.normal(ks[0], (BATCH, SEQ, D_MODEL), f32)
    norm_in_g = 1.0 + 0.05 * jax.random.normal(ks[1], (D_MODEL,), f32)
    w_in = jax.random.normal(ks[2], (D_MODEL, IN_WIDTH), f32) * D_MODEL ** -0.5
    conv_w = jax.random.normal(ks[3], (CONV_WIDTH, LRU_WIDTH), f32) * CONV_WIDTH ** -0.5
    conv_b = 0.01 * jax.random.normal(ks[4], (LRU_WIDTH,), f32)
    gate_a_w = jax.random.normal(ks[5], (LRU_BLOCKS, LRU_BLOCK_DIM, LRU_BLOCK_DIM), f32) * LRU_BLOCK_DIM ** -0.5
    gate_a_b = 0.01 * jax.random.normal(ks[6], (LRU_BLOCKS, LRU_BLOCK_DIM), f32)
    gate_x_w = jax.random.normal(ks[7], (LRU_BLOCKS, LRU_BLOCK_DIM, LRU_BLOCK_DIM), f32) * LRU_BLOCK_DIM ** -0.5
    gate_x_b = 0.01 * jax.random.normal(ks[8], (LRU_BLOCKS, LRU_BLOCK_DIM), f32)
    a_c = jax.random.uniform(ks[9], (LRU_WIDTH,), f32, minval=0.9, maxval=0.999)
    a0 = a_c ** (1.0 / LRU_C)
    lru_lambda = jnp.log(a0) - jnp.log1p(-a0)
    w_out = jax.random.normal(ks[10], (MIX_WIDTH, D_MODEL), f32) * MIX_WIDTH ** -0.5
    norm_out_g = 1.0 + 0.05 * jax.random.normal(ks[11], (D_MODEL,), f32)
    return {"x": x, "norm_in_g": norm_in_g, "w_in": w_in, "conv_w": conv_w, "conv_b": conv_b,
            "gate_a_w": gate_a_w, "gate_a_b": gate_a_b, "gate_x_w": gate_x_w, "gate_x_b": gate_x_b,
            "lru_lambda": lru_lambda, "w_out": w_out, "norm_out_g": norm_out_g}


def reference(x, norm_in_g, w_in, conv_w, conv_b, gate_a_w, gate_a_b, gate_x_w, gate_x_b,
              lru_lambda, w_out, norm_out_g):
    B, T, _ = x.shape
    f32 = jnp.float32
    h = x
    for _layer in range(DEPTH):
        xn = rms_norm(h, norm_in_g)
        proj = jnp.einsum('btd,de->bte', xn, w_in).astype(f32)
        q, k, v, g_ret, x_lru, g_lru = jnp.split(proj, IN_SPLITS, axis=-1)

        q = rotary(q.reshape(B, T, RET_HEADS, RET_HEAD_DIM))
        k = rotary(k.reshape(B, T, RET_HEADS, RET_HEAD_DIM)) * RET_HEAD_DIM ** -0.5
        v = v.reshape(B, T, RET_HEADS, RET_HEAD_DIM)
        o_ret = group_norm_heads(chunkwise_retention(q, k, v)).reshape(B, T, RET_WIDTH)
        y_ret = o_ret * jax.nn.silu(g_ret)

        xc = causal_depthwise_conv(x_lru, conv_w.astype(f32), conv_b.astype(f32))
        o_lru = rg_lru(xc, gate_a_w.astype(f32), gate_a_b.astype(f32), gate_x_w.astype(f32),
                       gate_x_b.astype(f32), lru_lambda.astype(f32))
        y_lru = o_lru * jax.nn.silu(g_lru)

        mixed = jnp.concatenate([y_ret, y_lru], axis=-1).astype(x.dtype)
        h = h + jnp.einsum('bte,ed->btd', mixed, w_out)
    return rms_norm(h, norm_out_g)
```

```python
import functools

import jax
import jax.numpy as jnp
from jax import lax
from jax.experimental import pallas as pl
from jax.experimental.pallas import tpu as pltpu

D_MODEL = 1024
RET_HEADS = 8
HEAD_DIM = 128
RET_CHUNK = 128
ROPE_BASE = 10000.0
LRU_BLOCKS = 8
LRU_BLOCK_DIM = 128
LRU_C = 8.0
CONV_WIDTH = 4
MIX_WIDTH = 2 * D_MODEL
IN_WIDTH = 6 * D_MODEL
NORM_EPS = 1e-6

V7X_SUBLANES = 8
TIME_BLOCK = 256
V7X_VMEM_LIMIT_BYTES = 56 * 1024 * 1024

F32 = jnp.float32
BF16 = jnp.bfloat16


def _sigmoid(z):
    return 1.0 / (1.0 + jnp.exp(-z))


def _rotate_half_apply(xh, cos_t, sin_t):
    return xh * cos_t + pltpu.roll(xh, HEAD_DIM // 2, 1) * sin_t


def _linear_scan(a, b, h0):
    rows = a.shape[0]
    groups = rows // V7X_SUBLANES
    a3 = a.reshape(groups, V7X_SUBLANES, a.shape[1])
    b3 = b.reshape(groups, V7X_SUBLANES, b.shape[1])
    row = lax.broadcasted_iota(jnp.int32, a3.shape, 1)
    shift = 1
    while shift < V7X_SUBLANES:
        keep = row >= shift
        a_prev = jnp.where(keep, pltpu.roll(a3, shift, 1), 1.0)
        b_prev = jnp.where(keep, pltpu.roll(b3, shift, 1), 0.0)
        b3 = a3 * b_prev + b3
        a3 = a3 * a_prev
        shift *= 2
    carry = h0
    out = []
    for g in range(groups):
        hg = b3[g] + a3[g] * carry
        out.append(hg)
        carry = hg[V7X_SUBLANES - 1:V7X_SUBLANES, :]
    return jnp.concatenate(out, axis=0)


def _layer_body(x_ref, cos_ref, sin_ref, gin_ref, win_ref, convw_ref, convb_ref, wg_ref,
                ba_ref, bx_ref, lam_ref, wout_ref, gout_ref, dmat_ref, xi_ref, zeta_ref,
                gdec_ref, out_ref, state_scr, hcarry_scr, tail_scr, mixed_scr):
    tb = x_ref.shape[1]
    n_chunks = tb // RET_CHUNK

    @pl.when(pl.program_id(1) == 0)
    def _():
        state_scr[...] = jnp.zeros_like(state_scr)
        hcarry_scr[...] = jnp.zeros_like(hcarry_scr)
        tail_scr[...] = jnp.zeros_like(tail_scr)

    x = x_ref[0]
    ms = jnp.mean(x * x, axis=-1, keepdims=True)
    xn = (x * lax.rsqrt(ms + NORM_EPS) * gin_ref[...]).astype(BF16)

    def proj(j):
        return jnp.dot(xn, win_ref[:, j * D_MODEL:(j + 1) * D_MODEL],
                       preferred_element_type=F32)

    q, k, v, g_ret, x_lru, g_lru = (proj(j) for j in range(6))
    cos_t = cos_ref[...]
    sin_t = sin_ref[...]

    for h in range(RET_HEADS):
        sl = slice(h * HEAD_DIM, (h + 1) * HEAD_DIM)
        qr = _rotate_half_apply(q[:, sl], cos_t, sin_t)
        kr = _rotate_half_apply(k[:, sl], cos_t, sin_t)
        state = state_scr[h]
        for c in range(n_chunks):
            rs = slice(c * RET_CHUNK, (c + 1) * RET_CHUNK)
            qc, kc = qr[rs], kr[rs]
            vc = v[rs, sl].astype(BF16)
            scores = lax.dot_general(qc.astype(BF16), kc.astype(BF16),
                                     (((1,), (1,)), ((), ())), preferred_element_type=F32)
            lhs = jnp.concatenate([(scores * dmat_ref[h]).astype(BF16),
                                   (qc * xi_ref[:, sl]).astype(BF16)], axis=1)
            rhs = jnp.concatenate([vc, state.astype(BF16)], axis=0)
            o = jnp.dot(lhs, rhs, preferred_element_type=F32)
            kz = (kc * zeta_ref[:, sl]).astype(BF16)
            kv = lax.dot_general(kz, vc, (((0,), (0,)), ((), ())), preferred_element_type=F32)
            state = gdec_ref[:, sl] * state + kv
            mu = jnp.mean(o, axis=-1, keepdims=True)
            oc = o - mu
            var = jnp.mean(oc * oc, axis=-1, keepdims=True)
            gate = g_ret[rs, sl]
            y = oc * lax.rsqrt(var + NORM_EPS) * (gate * _sigmoid(gate))
            mixed_scr[rs, sl] = y.astype(BF16)
        state_scr[h] = state

    ext = jnp.concatenate([tail_scr[...], x_lru], axis=0)
    tail_scr[...] = x_lru[tb - V7X_SUBLANES:tb]
    xc = convb_ref[...]
    for j in range(CONV_WIDTH):
        off = V7X_SUBLANES - (CONV_WIDTH - 1) + j
        xc = xc + ext[off:off + tb] * convw_ref[j:j + 1, :]
    xcb = xc.astype(BF16)
    z = -lam_ref[...]
    softplus = jnp.maximum(z, 0.0) + jnp.log1p(jnp.exp(-jnp.abs(z)))
    neg_c_softplus = -LRU_C * softplus
    for n in range(LRU_BLOCKS):
        sl = slice(n * LRU_BLOCK_DIM, (n + 1) * LRU_BLOCK_DIM)
        pre = jnp.dot(xcb[:, sl], wg_ref[n], preferred_element_type=F32)
        r = _sigmoid(pre[:, :LRU_BLOCK_DIM] + ba_ref[:, sl])
        i = _sigmoid(pre[:, LRU_BLOCK_DIM:] + bx_ref[:, sl])
        a = jnp.exp(r * neg_c_softplus[:, sl])
        mult = jnp.sqrt(jnp.maximum(1.0 - a * a, 0.0))
        b = mult * (i * xc[:, sl])
        hseq = _linear_scan(a, b, hcarry_scr[:, sl])
        hcarry_scr[:, sl] = hseq[tb - 1:tb]
        gate = g_lru[:, sl]
        osl = slice(D_MODEL + n * LRU_BLOCK_DIM, D_MODEL + (n + 1) * LRU_BLOCK_DIM)
        mixed_scr[:, osl] = (hseq * (gate * _sigmoid(gate))).astype(BF16)

    hres = x + jnp.dot(mixed_scr[...], wout_ref[...], preferred_element_type=F32)
    ms2 = jnp.mean(hres * hres, axis=-1, keepdims=True)
    out_ref[0] = hres * lax.rsqrt(ms2 + NORM_EPS) * gout_ref[...]


def _position_tables(seq_len):
    half = jnp.arange(0, HEAD_DIM, 2, dtype=F32)
    inv_freq = ROPE_BASE ** (-half / HEAD_DIM)
    ang = jnp.arange(seq_len, dtype=F32)[:, None] * inv_freq[None, :]
    cos, sin = jnp.cos(ang), jnp.sin(ang)
    cos_t = jnp.concatenate([cos, cos], axis=-1)
    sin_t = jnp.concatenate([-sin, sin], axis=-1)
    return cos_t, sin_t


def _decay_tables():
    c = RET_CHUNK
    scale = HEAD_DIM ** -0.5
    log_g = jnp.log1p(-jnp.exp2(-5.0 - jnp.arange(RET_HEADS, dtype=F32)))
    idx = jnp.arange(c, dtype=F32)
    rel = idx[:, None] - idx[None, :]
    dmat = jnp.where(rel[None] >= 0,
                     jnp.exp(jnp.maximum(rel, 0.0)[None] * log_g[:, None, None]), 0.0) * scale
    zeta = jnp.exp((c - 1 - idx)[None, :] * log_g[:, None]) * scale
    xi = jnp.exp((idx + 1)[None, :] * log_g[:, None])
    gdec = jnp.exp(c * log_g)

    def rows_by_head_lanes(t):
        return jnp.repeat(t.T[:, :, None], HEAD_DIM, axis=2).reshape(c, RET_HEADS * HEAD_DIM)

    return (dmat, rows_by_head_lanes(xi), rows_by_head_lanes(zeta),
            jnp.repeat(gdec, HEAD_DIM)[None, :])


def _resident(shape):
    nd = len(shape)
    return pl.BlockSpec(shape, lambda b, t: (0,) * nd, pipeline_mode=pl.Buffered(1))


@jax.jit
def kernel(x, norm_in_g, w_in, conv_w, conv_b, gate_a_w, gate_a_b, gate_x_w, gate_x_b,
           lru_lambda, w_out, norm_out_g):
    batch, seq_len, d_model = x.shape
    assert d_model == D_MODEL and seq_len % TIME_BLOCK == 0
    assert w_in.shape == (D_MODEL, IN_WIDTH) and w_out.shape == (MIX_WIDTH, D_MODEL)
    tb = TIME_BLOCK

    cos_t, sin_t = _position_tables(seq_len)
    dmat, xi_t, zeta_t, gdec_t = _decay_tables()
    w_gates = jnp.concatenate([gate_a_w, gate_x_w], axis=-1).astype(BF16)
    row = lambda p: p.reshape(1, D_MODEL).astype(F32)

    operands = (
        x, cos_t, sin_t, row(norm_in_g), w_in.astype(BF16), conv_w.astype(F32), row(conv_b),
        w_gates, row(gate_a_b), row(gate_x_b), row(lru_lambda), w_out.astype(BF16),
        row(norm_out_g), dmat, xi_t, zeta_t, gdec_t,
    )
    in_specs = [
        pl.BlockSpec((1, tb, D_MODEL), lambda b, t: (b, t, 0)),
        pl.BlockSpec((tb, HEAD_DIM), lambda b, t: (t, 0)),
        pl.BlockSpec((tb, HEAD_DIM), lambda b, t: (t, 0)),
    ] + [_resident(op.shape) for op in operands[3:]]

    return pl.pallas_call(
        _layer_body,
        grid=(batch, seq_len // tb),
        in_specs=in_specs,
        out_specs=pl.BlockSpec((1, tb, D_MODEL), lambda b, t: (b, t, 0)),
        out_shape=jax.ShapeDtypeStruct(x.shape, x.dtype),
        scratch_shapes=[
            pltpu.VMEM((RET_HEADS, HEAD_DIM, HEAD_DIM), F32),
            pltpu.VMEM((1, D_MODEL), F32),
            pltpu.VMEM((V7X_SUBLANES, D_MODEL), F32),
            pltpu.VMEM((tb, MIX_WIDTH), BF16),
        ],
        compiler_params=pltpu.CompilerParams(
            dimension_semantics=("arbitrary", "arbitrary"),
            vmem_limit_bytes=V7X_VMEM_LIMIT_BYTES,
        ),
        name="hybrid_layer",
    )(*operands)
```

```python
import functools

import jax
import jax.numpy as jnp
from jax import lax
from jax.experimental import pallas as pl
from jax.experimental.pallas import tpu as pltpu

D_MODEL = 1024
RET_HEADS = 8
HEAD_DIM = 128
RET_CHUNK = 128
ROPE_BASE = 10000.0
LRU_BLOCKS = 8
LRU_BLOCK_DIM = 128
LRU_C = 8.0
CONV_WIDTH = 4
MIX_WIDTH = 2 * D_MODEL
IN_WIDTH = 6 * D_MODEL
NORM_EPS = 1e-6

V7X_SUBLANES = 8
HALF_BLOCK = 256
STEP_BLOCK = 2 * HALF_BLOCK
MXU_PIECE_COLS = 512
V7X_VMEM_LIMIT_BYTES = 58 * 1024 * 1024

_W_IN_PIECE, _W_OUT_PIECE, _W_LRU_UNIT, _W_RET_UNIT = 512.0, 1024.0, 1400.0, 400.0

F32 = jnp.float32
BF16 = jnp.bfloat16


def _sigmoid(z):
    return 1.0 / (1.0 + jnp.exp(-z))


def _rotate_half_apply(xh, cos_t, sin_t):
    return xh * cos_t + pltpu.roll(xh, HEAD_DIM // 2, 1) * sin_t


def _linear_scan(a, b, h0):
    rows = a.shape[0]
    groups = rows // V7X_SUBLANES
    a3 = a.reshape(groups, V7X_SUBLANES, a.shape[1])
    b3 = b.reshape(groups, V7X_SUBLANES, b.shape[1])
    row = lax.broadcasted_iota(jnp.int32, a3.shape, 1)
    shift = 1
    while shift < V7X_SUBLANES:
        keep = row >= shift
        a_prev = jnp.where(keep, pltpu.roll(a3, shift, 1), 1.0)
        b_prev = jnp.where(keep, pltpu.roll(b3, shift, 1), 0.0)
        b3 = a3 * b_prev + b3
        a3 = a3 * a_prev
        shift *= 2
    carry = h0
    out = []
    for g in range(groups):
        hg = b3[g] + a3[g] * carry
        out.append(hg)
        carry = hg[V7X_SUBLANES - 1:V7X_SUBLANES, :]
    return jnp.concatenate(out, axis=0)


def _unpack_bf16(packed):
    return pltpu.bitcast(packed, BF16)


def _merge_by_weight(vector_units, matrix_units):
    total_v = sum(w for w, _ in vector_units) or 1.0
    total_m = sum(w for w, _ in matrix_units) or 1.0
    order, i, j, done_v, done_m = [], 0, 0, 0.0, 0.0
    while i < len(vector_units) or j < len(matrix_units):
        if j < len(matrix_units) and (i >= len(vector_units) or done_m / total_m <= done_v / total_v):
            w, f = matrix_units[j]
            j, done_m = j + 1, done_m + w
        else:
            w, f = vector_units[i]
            i, done_v = i + 1, done_v + w
        order.append(f)
    return order


def _layer_body(x_ref, xnext_ref, cos_ref, sin_ref, gin_ref, win_ref, convw_ref, convb_ref,
                wg_ref, ba_ref, bx_ref, lam_ref, wout_ref, gout_ref, dmat_ref, xi_ref,
                zeta_ref, gdec_ref, out_ref, proj_a, proj_b, xn_scr, hres_scr, state_scr,
                hcarry_scr, tail_scr, mixed_scr, *, steps_per_row):
    hb = HALF_BLOCK
    n_chunks = hb // RET_CHUNK
    step = pl.program_id(0)

    def norm_input(x, slot):
        ms = jnp.mean(x * x, axis=-1, keepdims=True)
        xn_scr[slot] = (x * lax.rsqrt(ms + NORM_EPS) * gin_ref[...]).astype(BF16)

    def in_piece(slot, proj_ref, p):
        cols = slice(p * MXU_PIECE_COLS, (p + 1) * MXU_PIECE_COLS)
        proj_ref[:, cols] = jnp.dot(xn_scr[slot], _unpack_bf16(win_ref[:, cols]),
                                    preferred_element_type=F32)

    def in_pieces(slot, proj_ref):
        return [(_W_IN_PIECE, functools.partial(in_piece, slot, proj_ref, p))
                for p in range(IN_WIDTH // MXU_PIECE_COLS)]

    def out_piece(half, p):
        cols = slice(p * MXU_PIECE_COLS, (p + 1) * MXU_PIECE_COLS)
        rows_h = slice(half * hb, (half + 1) * hb)
        hres_scr[half, :, cols] = x_ref[0, rows_h, cols] + jnp.dot(
            mixed_scr[half], _unpack_bf16(wout_ref[:, cols]), preferred_element_type=F32)

    def norm_output(half):
        rows_h = slice(half * hb, (half + 1) * hb)
        hres = hres_scr[half]
        ms = jnp.mean(hres * hres, axis=-1, keepdims=True)
        out_ref[0, rows_h, :] = hres * lax.rsqrt(ms + NORM_EPS) * gout_ref[...]

    def out_pieces(half):
        n = D_MODEL // MXU_PIECE_COLS
        units = [(_W_OUT_PIECE, functools.partial(out_piece, half, p)) for p in range(n - 1)]

        def last():
            out_piece(half, n - 1)
            norm_output(half)

        return units + [(_W_OUT_PIECE, last)]

    def col(proj_ref, j, rows, lanes):
        return proj_ref[rows, slice(j * D_MODEL + lanes.start, j * D_MODEL + lanes.stop)]

    def retention_unit(proj_ref, half, h, c):
        sl = slice(h * HEAD_DIM, (h + 1) * HEAD_DIM)
        rs = slice(c * RET_CHUNK, (c + 1) * RET_CHUNK)
        trows = slice(half * hb + c * RET_CHUNK, half * hb + (c + 1) * RET_CHUNK)
        cos_t, sin_t = cos_ref[trows, :], sin_ref[trows, :]
        qc = _rotate_half_apply(col(proj_ref, 0, rs, sl), cos_t, sin_t)
        kc = _rotate_half_apply(col(proj_ref, 1, rs, sl), cos_t, sin_t)
        vc = col(proj_ref, 2, rs, sl).astype(BF16)
        state = state_scr[h]
        scores = lax.dot_general(qc.astype(BF16), kc.astype(BF16),
                                 (((1,), (1,)), ((), ())), preferred_element_type=F32)
        lhs = jnp.concatenate([(scores * dmat_ref[h]).astype(BF16),
                               (qc * xi_ref[:, sl]).astype(BF16)], axis=1)
        rhs = jnp.concatenate([vc, state.astype(BF16)], axis=0)
        o = jnp.dot(lhs, rhs, preferred_element_type=F32)
        kz = (kc * zeta_ref[:, sl]).astype(BF16)
        kv = lax.dot_general(kz, vc, (((0,), (0,)), ((), ())), preferred_element_type=F32)
        state_scr[h] = gdec_ref[:, sl] * state + kv
        mu = jnp.mean(o, axis=-1, keepdims=True)
        oc = o - mu
        var = jnp.mean(oc * oc, axis=-1, keepdims=True)
        gate = col(proj_ref, 3, rs, sl)
        y = oc * lax.rsqrt(var + NORM_EPS) * (gate * _sigmoid(gate))
        mixed_scr[half, rs, sl] = y.astype(BF16)

    def lru_unit(proj_ref, half, n):
        sl = slice(n * LRU_BLOCK_DIM, (n + 1) * LRU_BLOCK_DIM)
        rows = slice(0, hb)
        x_lru = col(proj_ref, 4, rows, sl)
        ext = jnp.concatenate([tail_scr[:, sl], x_lru], axis=0)
        tail_scr[:, sl] = x_lru[hb - V7X_SUBLANES:hb]
        xc = convb_ref[:, sl]
        for j in range(CONV_WIDTH):
            off = V7X_SUBLANES - (CONV_WIDTH - 1) + j
            xc = xc + ext[off:off + hb] * convw_ref[j:j + 1, sl]
        pre = jnp.dot(xc.astype(BF16), wg_ref[n], preferred_element_type=F32)
        z = -lam_ref[:, sl]
        softplus = jnp.maximum(z, 0.0) + jnp.log1p(jnp.exp(-jnp.abs(z)))
        r = _sigmoid(pre[:, :LRU_BLOCK_DIM] + ba_ref[:, sl])
        i = _sigmoid(pre[:, LRU_BLOCK_DIM:] + bx_ref[:, sl])
        a = jnp.exp(r * (-LRU_C * softplus))
        mult = jnp.sqrt(jnp.maximum(1.0 - a * a, 0.0))
        b = mult * (i * xc)
        hseq = _linear_scan(a, b, hcarry_scr[:, sl])
        hcarry_scr[:, sl] = hseq[hb - 1:hb]
        gate = col(proj_ref, 5, rows, sl)
        osl = slice(D_MODEL + n * LRU_BLOCK_DIM, D_MODEL + (n + 1) * LRU_BLOCK_DIM)
        mixed_scr[half, :, osl] = (hseq * (gate * _sigmoid(gate))).astype(BF16)

    def mixer_units(proj_ref, half):
        units = []
        for n in range(LRU_BLOCKS):
            units.append((_W_LRU_UNIT, functools.partial(lru_unit, proj_ref, half, n)))
            for c in range(n_chunks):
                units.append((_W_RET_UNIT, functools.partial(retention_unit, proj_ref, half, n, c)))
        return units

    @pl.when(step == 0)
    def _():
        norm_input(x_ref[0, 0:hb, :], 0)
        for _, piece in in_pieces(0, proj_a):
            piece()

    @pl.when(step % steps_per_row == 0)
    def _():
        state_scr[...] = jnp.zeros_like(state_scr)
        hcarry_scr[...] = jnp.zeros_like(hcarry_scr)
        tail_scr[...] = jnp.zeros_like(tail_scr)

    norm_input(x_ref[0, hb:2 * hb, :], 1)
    for thunk in _merge_by_weight(mixer_units(proj_a, 0), in_pieces(1, proj_b)):
        thunk()
    norm_input(xnext_ref[0], 0)
    for thunk in _merge_by_weight(mixer_units(proj_b, 1), out_pieces(0) + in_pieces(0, proj_a)):
        thunk()
    for _, piece in out_pieces(1):
        piece()


def _position_tables(seq_len):
    half = jnp.arange(0, HEAD_DIM, 2, dtype=F32)
    inv_freq = ROPE_BASE ** (-half / HEAD_DIM)
    ang = jnp.arange(seq_len, dtype=F32)[:, None] * inv_freq[None, :]
    cos, sin = jnp.cos(ang), jnp.sin(ang)
    cos_t = jnp.concatenate([cos, cos], axis=-1)
    sin_t = jnp.concatenate([-sin, sin], axis=-1)
    return cos_t, sin_t


def _decay_tables():
    c = RET_CHUNK
    scale = HEAD_DIM ** -0.5
    log_g = jnp.log1p(-jnp.exp2(-5.0 - jnp.arange(RET_HEADS, dtype=F32)))
    idx = jnp.arange(c, dtype=F32)
    rel = idx[:, None] - idx[None, :]
    dmat = jnp.where(rel[None] >= 0,
                     jnp.exp(jnp.maximum(rel, 0.0)[None] * log_g[:, None, None]), 0.0) * scale
    zeta = jnp.exp((c - 1 - idx)[None, :] * log_g[:, None]) * scale
    xi = jnp.exp((idx + 1)[None, :] * log_g[:, None])
    gdec = jnp.exp(c * log_g)

    def rows_by_head_lanes(t):
        return jnp.repeat(t.T[:, :, None], HEAD_DIM, axis=2).reshape(c, RET_HEADS * HEAD_DIM)

    return (dmat, rows_by_head_lanes(xi), rows_by_head_lanes(zeta),
            jnp.repeat(gdec, HEAD_DIM)[None, :])


def _pack_bf16_rows(w):
    k, n = w.shape
    pairs = w.astype(BF16).reshape(k // 2, 2, n)
    return lax.bitcast_convert_type(jnp.swapaxes(pairs, 1, 2), jnp.uint32)


def _resident(shape):
    nd = len(shape)
    return pl.BlockSpec(shape, lambda i: (0,) * nd, pipeline_mode=pl.Buffered(1))


@jax.jit
def kernel(x, norm_in_g, w_in, conv_w, conv_b, gate_a_w, gate_a_b, gate_x_w, gate_x_b,
           lru_lambda, w_out, norm_out_g):
    batch, seq_len, d_model = x.shape
    assert d_model == D_MODEL and seq_len % STEP_BLOCK == 0
    assert w_in.shape == (D_MODEL, IN_WIDTH) and w_out.shape == (MIX_WIDTH, D_MODEL)
    steps_per_row = seq_len // STEP_BLOCK
    n_steps = batch * steps_per_row
    halves_per_row = 2 * steps_per_row

    cos_t, sin_t = _position_tables(seq_len)
    dmat, xi_t, zeta_t, gdec_t = _decay_tables()
    w_gates = jnp.concatenate([gate_a_w, gate_x_w], axis=-1).astype(BF16)
    row = lambda p: p.reshape(1, D_MODEL).astype(F32)

    operands = (
        x, x, cos_t, sin_t, row(norm_in_g), _pack_bf16_rows(w_in), conv_w.astype(F32),
        row(conv_b), w_gates, row(gate_a_b), row(gate_x_b), row(lru_lambda),
        _pack_bf16_rows(w_out), row(norm_out_g), dmat, xi_t, zeta_t, gdec_t,
    )

    def next_half(i):
        n = jnp.minimum(2 * (i + 1), 2 * n_steps - 2)
        return (n // halves_per_row, n % halves_per_row, 0)

    in_specs = [
        pl.BlockSpec((1, STEP_BLOCK, D_MODEL), lambda i: (i // steps_per_row, i % steps_per_row, 0)),
        pl.BlockSpec((1, HALF_BLOCK, D_MODEL), next_half),
        pl.BlockSpec((STEP_BLOCK, HEAD_DIM), lambda i: (i % steps_per_row, 0)),
        pl.BlockSpec((STEP_BLOCK, HEAD_DIM), lambda i: (i % steps_per_row, 0)),
    ] + [_resident(op.shape) for op in operands[4:]]

    return pl.pallas_call(
        functools.partial(_layer_body, steps_per_row=steps_per_row),
        grid=(n_steps,),
        in_specs=in_specs,
        out_specs=pl.BlockSpec((1, STEP_BLOCK, D_MODEL),
                               lambda i: (i // steps_per_row, i % steps_per_row, 0)),
        out_shape=jax.ShapeDtypeStruct(x.shape, x.dtype),
        scratch_shapes=[
            pltpu.VMEM((HALF_BLOCK, IN_WIDTH), F32),
            pltpu.VMEM((HALF_BLOCK, IN_WIDTH), F32),
            pltpu.VMEM((2, HALF_BLOCK, D_MODEL), BF16),
            pltpu.VMEM((2, HALF_BLOCK, D_MODEL), F32),
            pltpu.VMEM((RET_HEADS, HEAD_DIM, HEAD_DIM), F32),
            pltpu.VMEM((1, D_MODEL), F32),
            pltpu.VMEM((V7X_SUBLANES, D_MODEL), F32),
            pltpu.VMEM((2, HALF_BLOCK, MIX_WIDTH), BF16),
        ],
        compiler_params=pltpu.CompilerParams(
            dimension_semantics=("arbitrary",),
            vmem_limit_bytes=V7X_VMEM_LIMIT_BYTES,
        ),
        name="hybrid_layer",
    )(*operands)
```

```python
import functools

import jax
import jax.numpy as jnp
from jax import lax
from jax.experimental import pallas as pl
from jax.experimental.pallas import tpu as pltpu

D_MODEL = 1024
RET_HEADS = 8
HEAD_DIM = 128
RET_CHUNK = 128
ROPE_BASE = 10000.0
LRU_BLOCKS = 8
LRU_BLOCK_DIM = 128
LRU_C = 8.0
CONV_WIDTH = 4
MIX_WIDTH = 2 * D_MODEL
IN_WIDTH = 6 * D_MODEL
NORM_EPS = 1e-6

V7X_SUBLANES = 8
HALF_BLOCK = 256
STEP_BLOCK = 2 * HALF_BLOCK
MXU_PIECE_COLS = 512
LRU_ROWS = RET_CHUNK
V7X_VMEM_LIMIT_BYTES = 58 * 1024 * 1024

_W_IN_PIECE, _W_OUT_PIECE, _W_LRU_UNIT, _W_RET_UNIT = 512.0, 1024.0, 320.0, 200.0

F32 = jnp.float32
BF16 = jnp.bfloat16
LOG2_E = 1.4426950408889634


def _sigmoid(z):
    return 1.0 / (1.0 + jnp.exp2(z * (-LOG2_E)))


def _rotate_half_apply(xh, cos_t, sin_t):
    return xh * cos_t + pltpu.roll(xh, HEAD_DIM // 2, 1) * sin_t


def _linear_scan(a, b, h0):
    rows = a.shape[0]
    groups = rows // V7X_SUBLANES
    a3 = a.reshape(groups, V7X_SUBLANES, a.shape[1])
    b3 = b.reshape(groups, V7X_SUBLANES, b.shape[1])
    row = lax.broadcasted_iota(jnp.int32, a3.shape, 1)
    shift = 1
    while shift < V7X_SUBLANES:
        keep = row >= shift
        a_prev = jnp.where(keep, pltpu.roll(a3, shift, 1), 1.0)
        b_prev = jnp.where(keep, pltpu.roll(b3, shift, 1), 0.0)
        b3 = a3 * b_prev + b3
        a3 = a3 * a_prev
        shift *= 2
    carry = h0
    out = []
    for g in range(groups):
        hg = b3[g] + a3[g] * carry
        out.append(hg)
        carry = hg[V7X_SUBLANES - 1:V7X_SUBLANES, :]
    return jnp.concatenate(out, axis=0)


def _unpack_bf16(packed):
    return pltpu.bitcast(packed, BF16)


def _merge_by_weight(vector_units, matrix_units):
    total_v = sum(w for w, _ in vector_units) or 1.0
    total_m = sum(w for w, _ in matrix_units) or 1.0
    order, i, j, done_v, done_m = [], 0, 0, 0.0, 0.0
    while i < len(vector_units) or j < len(matrix_units):
        if j < len(matrix_units) and (i >= len(vector_units) or done_m / total_m <= done_v / total_v):
            w, f = matrix_units[j]
            j, done_m = j + 1, done_m + w
        else:
            w, f = vector_units[i]
            i, done_v = i + 1, done_v + w
        order.append(f)
    return order


def _layer_body(x_ref, xnext_ref, cos_ref, sin_ref, gin_ref, win_ref, convw_ref, convb_ref,
                wg_ref, ba_ref, bx_ref, lam_ref, wout_ref, gout_ref, dmat_ref, xi_ref,
                zeta_ref, gdec_ref, out_ref, proj_a, proj_b, xn_scr, hres_scr, state_scr,
                hcarry_scr, tail_scr, mixed_scr, *, steps_per_row):
    hb = HALF_BLOCK
    n_chunks = hb // RET_CHUNK
    step = pl.program_id(0)

    def norm_input(x, slot):
        ms = jnp.mean(x * x, axis=-1, keepdims=True)
        xn_scr[slot] = (x * lax.rsqrt(ms + NORM_EPS) * gin_ref[...]).astype(BF16)

    def in_piece(slot, proj_ref, p):
        cols = slice(p * MXU_PIECE_COLS, (p + 1) * MXU_PIECE_COLS)
        proj_ref[:, cols] = jnp.dot(xn_scr[slot], _unpack_bf16(win_ref[:, cols]),
                                    preferred_element_type=F32)

    def in_pieces(slot, proj_ref):
        return [(_W_IN_PIECE, functools.partial(in_piece, slot, proj_ref, p))
                for p in range(IN_WIDTH // MXU_PIECE_COLS)]

    def out_piece(half, p):
        cols = slice(p * MXU_PIECE_COLS, (p + 1) * MXU_PIECE_COLS)
        rows_h = slice(half * hb, (half + 1) * hb)
        hres_scr[half, :, cols] = x_ref[0, rows_h, cols] + jnp.dot(
            mixed_scr[half], _unpack_bf16(wout_ref[:, cols]), preferred_element_type=F32)

    def norm_output(half):
        rows_h = slice(half * hb, (half + 1) * hb)
        hres = hres_scr[half]
        ms = jnp.mean(hres * hres, axis=-1, keepdims=True)
        out_ref[0, rows_h, :] = hres * lax.rsqrt(ms + NORM_EPS) * gout_ref[...]

    def out_pieces(half):
        n = D_MODEL // MXU_PIECE_COLS
        units = [(_W_OUT_PIECE, functools.partial(out_piece, half, p)) for p in range(n - 1)]

        def last():
            out_piece(half, n - 1)
            norm_output(half)

        return units + [(_W_OUT_PIECE, last)]

    def col(proj_ref, j, rows, lanes):
        return proj_ref[rows, slice(j * D_MODEL + lanes.start, j * D_MODEL + lanes.stop)]

    def retention_unit(proj_ref, half, h, c):
        sl = slice(h * HEAD_DIM, (h + 1) * HEAD_DIM)
        rs = slice(c * RET_CHUNK, (c + 1) * RET_CHUNK)
        trows = slice(half * hb + c * RET_CHUNK, half * hb + (c + 1) * RET_CHUNK)
        cos_t, sin_t = cos_ref[trows, :], sin_ref[trows, :]
        qc = _rotate_half_apply(col(proj_ref, 0, rs, sl), cos_t, sin_t)
        kc = _rotate_half_apply(col(proj_ref, 1, rs, sl), cos_t, sin_t)
        vc = col(proj_ref, 2, rs, sl).astype(BF16)
        state = state_scr[h]
        scores = lax.dot_general(qc.astype(BF16), kc.astype(BF16),
                                 (((1,), (1,)), ((), ())), preferred_element_type=F32)
        lhs = jnp.concatenate([(scores * dmat_ref[h]).astype(BF16),
                               (qc * xi_ref[:, sl]).astype(BF16)], axis=1)
        rhs = jnp.concatenate([vc, state.astype(BF16)], axis=0)
        o = jnp.dot(lhs, rhs, preferred_element_type=F32)
        kz = (kc * zeta_ref[:, sl]).astype(BF16)
        kv = lax.dot_general(kz, vc, (((0,), (0,)), ((), ())), preferred_element_type=F32)
        state_scr[h] = gdec_ref[:, sl] * state + kv
        mu = jnp.mean(o, axis=-1, keepdims=True)
        oc = o - mu
        var = jnp.mean(oc * oc, axis=-1, keepdims=True)
        gate = col(proj_ref, 3, rs, sl)
        y = oc * lax.rsqrt(var + NORM_EPS) * (gate * _sigmoid(gate))
        mixed_scr[half, rs, sl] = y.astype(BF16)

    def lru_unit(proj_ref, half, n, rh):
        sl = slice(n * LRU_BLOCK_DIM, (n + 1) * LRU_BLOCK_DIM)
        rows = slice(rh * LRU_ROWS, (rh + 1) * LRU_ROWS)
        groups = LRU_ROWS // V7X_SUBLANES
        x_lru = col(proj_ref, 4, rows, sl)
        if rh == 0:
            prev = tail_scr[:, sl]
        else:
            prev = col(proj_ref, 4, slice(rows.start - V7X_SUBLANES, rows.start), sl)
        if rh == hb // LRU_ROWS - 1:
            tail_scr[:, sl] = x_lru[LRU_ROWS - V7X_SUBLANES:LRU_ROWS]
        x3 = x_lru.reshape(groups, V7X_SUBLANES, LRU_BLOCK_DIM)
        all3 = jnp.concatenate([prev[None], x3], axis=0)
        sub = lax.broadcasted_iota(jnp.int32, x3.shape, 1)
        xc3 = convb_ref[:, sl] + x3 * convw_ref[CONV_WIDTH - 1:CONV_WIDTH, sl]
        for s in range(1, CONV_WIDTH):
            rolled = pltpu.roll(all3, s, 1)
            shifted = jnp.where(sub >= s, rolled[1:], rolled[:-1])
            xc3 = xc3 + shifted * convw_ref[CONV_WIDTH - 1 - s:CONV_WIDTH - s, sl]
        xc = xc3.reshape(LRU_ROWS, LRU_BLOCK_DIM)
        pre = jnp.dot(xc.astype(BF16), wg_ref[n], preferred_element_type=F32)
        z = -lam_ref[:, sl]
        softplus = jnp.maximum(z, 0.0) + jnp.log1p(jnp.exp(-jnp.abs(z)))
        r = _sigmoid(pre[:, :LRU_BLOCK_DIM] + ba_ref[:, sl])
        i = _sigmoid(pre[:, LRU_BLOCK_DIM:] + bx_ref[:, sl])
        a = jnp.exp2(r * ((-LRU_C * LOG2_E) * softplus))
        v = 1.0 - a * a
        mult = jnp.where(v > 0.0, v * lax.rsqrt(v), 0.0)
        b = mult * (i * xc)
        hseq = _linear_scan(a, b, hcarry_scr[:, sl])
        hcarry_scr[:, sl] = hseq[LRU_ROWS - 1:LRU_ROWS]
        gate = col(proj_ref, 5, rows, sl)
        osl = slice(D_MODEL + n * LRU_BLOCK_DIM, D_MODEL + (n + 1) * LRU_BLOCK_DIM)
        mixed_scr[half, rows, osl] = (hseq * (gate * _sigmoid(gate))).astype(BF16)

    def mixer_units(proj_ref, half):
        units = []
        for n in range(LRU_BLOCKS):
            for c in range(n_chunks):
                units.append((_W_LRU_UNIT, functools.partial(lru_unit, proj_ref, half, n, c)))
                units.append((_W_RET_UNIT, functools.partial(retention_unit, proj_ref, half, n, c)))
        return units

    @pl.when(step == 0)
    def _():
        norm_input(x_ref[0, 0:hb, :], 0)
        for _, piece in in_pieces(0, proj_a):
            piece()

    @pl.when(step % steps_per_row == 0)
    def _():
        state_scr[...] = jnp.zeros_like(state_scr)
        hcarry_scr[...] = jnp.zeros_like(hcarry_scr)
        tail_scr[...] = jnp.zeros_like(tail_scr)

    norm_input(x_ref[0, hb:2 * hb, :], 1)
    for thunk in _merge_by_weight(mixer_units(proj_a, 0), in_pieces(1, proj_b)):
        thunk()
    norm_input(xnext_ref[0], 0)
    for thunk in _merge_by_weight(mixer_units(proj_b, 1), out_pieces(0) + in_pieces(0, proj_a)):
        thunk()
    for _, piece in out_pieces(1):
        piece()


def _position_tables(seq_len):
    half = jnp.arange(0, HEAD_DIM, 2, dtype=F32)
    inv_freq = ROPE_BASE ** (-half / HEAD_DIM)
    ang = jnp.arange(seq_len, dtype=F32)[:, None] * inv_freq[None, :]
    cos, sin = jnp.cos(ang), jnp.sin(ang)
    cos_t = jnp.concatenate([cos, cos], axis=-1)
    sin_t = jnp.concatenate([-sin, sin], axis=-1)
    return cos_t, sin_t


def _decay_tables():
    c = RET_CHUNK
    scale = HEAD_DIM ** -0.5
    log_g = jnp.log1p(-jnp.exp2(-5.0 - jnp.arange(RET_HEADS, dtype=F32)))
    idx = jnp.arange(c, dtype=F32)
    rel = idx[:, None] - idx[None, :]
    dmat = jnp.where(rel[None] >= 0,
                     jnp.exp(jnp.maximum(rel, 0.0)[None] * log_g[:, None, None]), 0.0) * scale
    zeta = jnp.exp((c - 1 - idx)[None, :] * log_g[:, None]) * scale
    xi = jnp.exp((idx + 1)[None, :] * log_g[:, None])
    gdec = jnp.exp(c * log_g)

    def rows_by_head_lanes(t):
        return jnp.repeat(t.T[:, :, None], HEAD_DIM, axis=2).reshape(c, RET_HEADS * HEAD_DIM)

    return (dmat, rows_by_head_lanes(xi), rows_by_head_lanes(zeta),
            jnp.repeat(gdec, HEAD_DIM)[None, :])


def _pack_bf16_rows(w):
    k, n = w.shape
    bits = lax.bitcast_convert_type(w.astype(BF16), jnp.uint16).astype(jnp.uint32)
    bits = bits.reshape(k // 2, 2, n)
    return bits[:, 0, :] | (bits[:, 1, :] << 16)


def _resident(shape):
    nd = len(shape)
    return pl.BlockSpec(shape, lambda i: (0,) * nd, pipeline_mode=pl.Buffered(1))


@jax.jit
def kernel(x, norm_in_g, w_in, conv_w, conv_b, gate_a_w, gate_a_b, gate_x_w, gate_x_b,
           lru_lambda, w_out, norm_out_g):
    batch, seq_len, d_model = x.shape
    assert d_model == D_MODEL and seq_len % STEP_BLOCK == 0
    assert w_in.shape == (D_MODEL, IN_WIDTH) and w_out.shape == (MIX_WIDTH, D_MODEL)
    steps_per_row = seq_len // STEP_BLOCK
    n_steps = batch * steps_per_row
    halves_per_row = 2 * steps_per_row

    cos_t, sin_t = _position_tables(seq_len)
    dmat, xi_t, zeta_t, gdec_t = _decay_tables()
    w_gates = jnp.concatenate([gate_a_w, gate_x_w], axis=-1).astype(BF16)
    row = lambda p: p.reshape(1, D_MODEL).astype(F32)

    operands = (
        x, x, cos_t, sin_t, row(norm_in_g), _pack_bf16_rows(w_in), conv_w.astype(F32),
        row(conv_b), w_gates, row(gate_a_b), row(gate_x_b), row(lru_lambda),
        _pack_bf16_rows(w_out), row(norm_out_g), dmat, xi_t, zeta_t, gdec_t,
    )

    def next_half(i):
        n = jnp.minimum(2 * (i + 1), 2 * n_steps - 2)
        return (n // halves_per_row, n % halves_per_row, 0)

    in_specs = [
        pl.BlockSpec((1, STEP_BLOCK, D_MODEL), lambda i: (i // steps_per_row, i % steps_per_row, 0)),
        pl.BlockSpec((1, HALF_BLOCK, D_MODEL), next_half),
        pl.BlockSpec((STEP_BLOCK, HEAD_DIM), lambda i: (i % steps_per_row, 0)),
        pl.BlockSpec((STEP_BLOCK, HEAD_DIM), lambda i: (i % steps_per_row, 0)),
    ] + [_resident(op.shape) for op in operands[4:]]

    return pl.pallas_call(
        functools.partial(_layer_body, steps_per_row=steps_per_row),
        grid=(n_steps,),
        in_specs=in_specs,
        out_specs=pl.BlockSpec((1, STEP_BLOCK, D_MODEL),
                               lambda i: (i // steps_per_row, i % steps_per_row, 0)),
        out_shape=jax.ShapeDtypeStruct(x.shape, x.dtype),
        scratch_shapes=[
            pltpu.VMEM((HALF_BLOCK, IN_WIDTH), F32),
            pltpu.VMEM((HALF_BLOCK, IN_WIDTH), F32),
            pltpu.VMEM((2, HALF_BLOCK, D_MODEL), BF16),
            pltpu.VMEM((2, HALF_BLOCK, D_MODEL), F32),
            pltpu.VMEM((RET_HEADS, HEAD_DIM, HEAD_DIM), F32),
            pltpu.VMEM((1, D_MODEL), F32),
            pltpu.VMEM((V7X_SUBLANES, D_MODEL), F32),
            pltpu.VMEM((2, HALF_BLOCK, MIX_WIDTH), BF16),
        ],
        compiler_params=pltpu.CompilerParams(
            dimension_semantics=("arbitrary",),
            vmem_limit_bytes=V7X_VMEM_LIMIT_BYTES,
        ),
        name="hybrid_layer",
    )(*operands)
```

```python
import functools

import jax
import jax.numpy as jnp
from jax import lax
from jax.experimental import pallas as pl
from jax.experimental.pallas import tpu as pltpu

D_MODEL = 1024
RET_HEADS = 8
HEAD_DIM = 128
RET_CHUNK = 128
ROPE_BASE = 10000.0
LRU_BLOCKS = 8
LRU_BLOCK_DIM = 128
LRU_C = 8.0
CONV_WIDTH = 4
MIX_WIDTH = 2 * D_MODEL
IN_WIDTH = 6 * D_MODEL
NORM_EPS = 1e-6

V7X_SUBLANES = 8
HALF_BLOCK = 256
STEP_BLOCK = 2 * HALF_BLOCK
MXU_PIECE_COLS = 512
LRU_ROWS = RET_CHUNK
WIN_STAGE_ROWS = 64
WOUT_STAGE_ROWS = 256
V7X_VMEM_LIMIT_BYTES = 58 * 1024 * 1024

_W_IN_PIECE, _W_OUT_PIECE, _W_LRU_UNIT, _W_RET_UNIT = 512.0, 1024.0, 320.0, 200.0

F32 = jnp.float32
BF16 = jnp.bfloat16
LOG2_E = 1.4426950408889634


def _sigmoid(z):
    return 1.0 / (1.0 + jnp.exp2(z * (-LOG2_E)))


def _rotate_half_apply(xh, cos_t, sin_t):
    return xh * cos_t + pltpu.roll(xh, HEAD_DIM // 2, 1) * sin_t


def _linear_scan(a, b, h0):
    rows = a.shape[0]
    groups = rows // V7X_SUBLANES
    a3 = a.reshape(groups, V7X_SUBLANES, a.shape[1])
    b3 = b.reshape(groups, V7X_SUBLANES, b.shape[1])
    row = lax.broadcasted_iota(jnp.int32, a3.shape, 1)
    shift = 1
    while shift < V7X_SUBLANES:
        keep = row >= shift
        a_prev = jnp.where(keep, pltpu.roll(a3, shift, 1), 1.0)
        b_prev = jnp.where(keep, pltpu.roll(b3, shift, 1), 0.0)
        b3 = a3 * b_prev + b3
        a3 = a3 * a_prev
        shift *= 2
    carry = h0
    out = []
    for g in range(groups):
        hg = b3[g] + a3[g] * carry
        out.append(hg)
        carry = hg[V7X_SUBLANES - 1:V7X_SUBLANES, :]
    return jnp.concatenate(out, axis=0)


def _load_weight_as_bf16(w_hbm, w_scr, stage, sems):
    chunk_rows = stage.shape[1]
    n_chunks = w_hbm.shape[0] // chunk_rows

    def chunk_copy(c, slot):
        return pltpu.make_async_copy(w_hbm.at[pl.ds(c * chunk_rows, chunk_rows), :],
                                     stage.at[slot], sems.at[slot])

    chunk_copy(0, 0).start()
    for c in range(n_chunks):
        slot = c % 2
        if c + 1 < n_chunks:
            chunk_copy(c + 1, 1 - slot).start()
        chunk_copy(c, slot).wait()
        w_scr[c * chunk_rows:(c + 1) * chunk_rows, :] = stage[slot].astype(BF16)


def _merge_by_weight(vector_units, matrix_units):
    total_v = sum(w for w, _ in vector_units) or 1.0
    total_m = sum(w for w, _ in matrix_units) or 1.0
    order, i, j, done_v, done_m = [], 0, 0, 0.0, 0.0
    while i < len(vector_units) or j < len(matrix_units):
        if j < len(matrix_units) and (i >= len(vector_units) or done_m / total_m <= done_v / total_v):
            w, f = matrix_units[j]
            j, done_m = j + 1, done_m + w
        else:
            w, f = vector_units[i]
            i, done_v = i + 1, done_v + w
        order.append(f)
    return order


def _layer_body(x_ref, xnext_ref, cos_ref, sin_ref, gin_ref, win_hbm, convw_ref, convb_ref,
                wg_ref, ba_ref, bx_ref, lam_ref, wout_hbm, gout_ref, dmat_ref, xi_ref,
                zeta_ref, gdec_ref, out_ref, win_scr, wout_scr, win_stage, wout_stage,
                win_sems, wout_sems, proj_a, proj_b, xn_scr, hres_scr, state_scr,
                hcarry_scr, tail_scr, mixed_scr, *, steps_per_row):
    hb = HALF_BLOCK
    n_chunks = hb // RET_CHUNK
    step = pl.program_id(0)

    def norm_input(x, slot):
        ms = jnp.mean(x * x, axis=-1, keepdims=True)
        xn_scr[slot] = (x * lax.rsqrt(ms + NORM_EPS) * gin_ref[...]).astype(BF16)

    def in_piece(slot, proj_ref, p):
        cols = slice(p * MXU_PIECE_COLS, (p + 1) * MXU_PIECE_COLS)
        proj_ref[:, cols] = jnp.dot(xn_scr[slot], win_scr[:, cols],
                                    preferred_element_type=F32)

    def in_pieces(slot, proj_ref):
        return [(_W_IN_PIECE, functools.partial(in_piece, slot, proj_ref, p))
                for p in range(IN_WIDTH // MXU_PIECE_COLS)]

    def out_piece(half, p):
        cols = slice(p * MXU_PIECE_COLS, (p + 1) * MXU_PIECE_COLS)
        rows_h = slice(half * hb, (half + 1) * hb)
        hres_scr[half, :, cols] = x_ref[0, rows_h, cols] + jnp.dot(
            mixed_scr[half], wout_scr[:, cols], preferred_element_type=F32)

    def norm_output(half):
        rows_h = slice(half * hb, (half + 1) * hb)
        hres = hres_scr[half]
        ms = jnp.mean(hres * hres, axis=-1, keepdims=True)
        out_ref[0, rows_h, :] = hres * lax.rsqrt(ms + NORM_EPS) * gout_ref[...]

    def out_pieces(half):
        n = D_MODEL // MXU_PIECE_COLS
        units = [(_W_OUT_PIECE, functools.partial(out_piece, half, p)) for p in range(n - 1)]

        def last():
            out_piece(half, n - 1)
            norm_output(half)

        return units + [(_W_OUT_PIECE, last)]

    def col(proj_ref, j, rows, lanes):
        return proj_ref[rows, slice(j * D_MODEL + lanes.start, j * D_MODEL + lanes.stop)]

    def retention_unit(proj_ref, half, h, c):
        sl = slice(h * HEAD_DIM, (h + 1) * HEAD_DIM)
        rs = slice(c * RET_CHUNK, (c + 1) * RET_CHUNK)
        trows = slice(half * hb + c * RET_CHUNK, half * hb + (c + 1) * RET_CHUNK)
        cos_t, sin_t = cos_ref[trows, :], sin_ref[trows, :]
        qc = _rotate_half_apply(col(proj_ref, 0, rs, sl), cos_t, sin_t)
        kc = _rotate_half_apply(col(proj_ref, 1, rs, sl), cos_t, sin_t)
        vc = col(proj_ref, 2, rs, sl).astype(BF16)
        state = state_scr[h]
        scores = lax.dot_general(qc.astype(BF16), kc.astype(BF16),
                                 (((1,), (1,)), ((), ())), preferred_element_type=F32)
        lhs = jnp.concatenate([(scores * dmat_ref[h]).astype(BF16),
                               (qc * xi_ref[:, sl]).astype(BF16)], axis=1)
        rhs = jnp.concatenate([vc, state.astype(BF16)], axis=0)
        o = jnp.dot(lhs, rhs, preferred_element_type=F32)
        kz = (kc * zeta_ref[:, sl]).astype(BF16)
        kv = lax.dot_general(kz, vc, (((0,), (0,)), ((), ())), preferred_element_type=F32)
        state_scr[h] = gdec_ref[:, sl] * state + kv
        mu = jnp.mean(o, axis=-1, keepdims=True)
        oc = o - mu
        var = jnp.mean(oc * oc, axis=-1, keepdims=True)
        gate = col(proj_ref, 3, rs, sl)
        y = oc * lax.rsqrt(var + NORM_EPS) * (gate * _sigmoid(gate))
        mixed_scr[half, rs, sl] = y.astype(BF16)

    def lru_unit(proj_ref, half, n, rh):
        sl = slice(n * LRU_BLOCK_DIM, (n + 1) * LRU_BLOCK_DIM)
        rows = slice(rh * LRU_ROWS, (rh + 1) * LRU_ROWS)
        groups = LRU_ROWS // V7X_SUBLANES
        x_lru = col(proj_ref, 4, rows, sl)
        if rh == 0:
            prev = tail_scr[:, sl]
        else:
            prev = col(proj_ref, 4, slice(rows.start - V7X_SUBLANES, rows.start), sl)
        if rh == hb // LRU_ROWS - 1:
            tail_scr[:, sl] = x_lru[LRU_ROWS - V7X_SUBLANES:LRU_ROWS]
        x3 = x_lru.reshape(groups, V7X_SUBLANES, LRU_BLOCK_DIM)
        all3 = jnp.concatenate([prev[None], x3], axis=0)
        sub = lax.broadcasted_iota(jnp.int32, x3.shape, 1)
        xc3 = convb_ref[:, sl] + x3 * convw_ref[CONV_WIDTH - 1:CONV_WIDTH, sl]
        for s in range(1, CONV_WIDTH):
            rolled = pltpu.roll(all3, s, 1)
            shifted = jnp.where(sub >= s, rolled[1:], rolled[:-1])
            xc3 = xc3 + shifted * convw_ref[CONV_WIDTH - 1 - s:CONV_WIDTH - s, sl]
        xc = xc3.reshape(LRU_ROWS, LRU_BLOCK_DIM)
        pre = jnp.dot(xc.astype(BF16), wg_ref[n], preferred_element_type=F32)
        z = -lam_ref[:, sl]
        softplus = jnp.maximum(z, 0.0) + jnp.log1p(jnp.exp(-jnp.abs(z)))
        r = _sigmoid(pre[:, :LRU_BLOCK_DIM] + ba_ref[:, sl])
        i = _sigmoid(pre[:, LRU_BLOCK_DIM:] + bx_ref[:, sl])
        a = jnp.exp2(r * ((-LRU_C * LOG2_E) * softplus))
        v = 1.0 - a * a
        mult = jnp.where(v > 0.0, v * lax.rsqrt(v), 0.0)
        b = mult * (i * xc)
        hseq = _linear_scan(a, b, hcarry_scr[:, sl])
        hcarry_scr[:, sl] = hseq[LRU_ROWS - 1:LRU_ROWS]
        gate = col(proj_ref, 5, rows, sl)
        osl = slice(D_MODEL + n * LRU_BLOCK_DIM, D_MODEL + (n + 1) * LRU_BLOCK_DIM)
        mixed_scr[half, rows, osl] = (hseq * (gate * _sigmoid(gate))).astype(BF16)

    def mixer_units(proj_ref, half):
        units = []
        for n in range(LRU_BLOCKS):
            for c in range(n_chunks):
                units.append((_W_LRU_UNIT, functools.partial(lru_unit, proj_ref, half, n, c)))
                units.append((_W_RET_UNIT, functools.partial(retention_unit, proj_ref, half, n, c)))
        return units

    @pl.when(step == 0)
    def _():
        _load_weight_as_bf16(win_hbm, win_scr, win_stage, win_sems)
        _load_weight_as_bf16(wout_hbm, wout_scr, wout_stage, wout_sems)
        norm_input(x_ref[0, 0:hb, :], 0)
        for _, piece in in_pieces(0, proj_a):
            piece()

    @pl.when(step % steps_per_row == 0)
    def _():
        state_scr[...] = jnp.zeros_like(state_scr)
        hcarry_scr[...] = jnp.zeros_like(hcarry_scr)
        tail_scr[...] = jnp.zeros_like(tail_scr)

    norm_input(x_ref[0, hb:2 * hb, :], 1)
    for thunk in _merge_by_weight(mixer_units(proj_a, 0), in_pieces(1, proj_b)):
        thunk()
    norm_input(xnext_ref[0], 0)
    for thunk in _merge_by_weight(mixer_units(proj_b, 1), out_pieces(0) + in_pieces(0, proj_a)):
        thunk()
    for _, piece in out_pieces(1):
        piece()


def _position_tables(seq_len):
    half = jnp.arange(0, HEAD_DIM, 2, dtype=F32)
    inv_freq = ROPE_BASE ** (-half / HEAD_DIM)
    ang = jnp.arange(seq_len, dtype=F32)[:, None] * inv_freq[None, :]
    cos, sin = jnp.cos(ang), jnp.sin(ang)
    cos_t = jnp.concatenate([cos, cos], axis=-1)
    sin_t = jnp.concatenate([-sin, sin], axis=-1)
    return cos_t, sin_t


def _decay_tables():
    c = RET_CHUNK
    scale = HEAD_DIM ** -0.5
    log_g = jnp.log1p(-jnp.exp2(-5.0 - jnp.arange(RET_HEADS, dtype=F32)))
    idx = jnp.arange(c, dtype=F32)
    rel = idx[:, None] - idx[None, :]
    dmat = jnp.where(rel[None] >= 0,
                     jnp.exp(jnp.maximum(rel, 0.0)[None] * log_g[:, None, None]), 0.0) * scale
    zeta = jnp.exp((c - 1 - idx)[None, :] * log_g[:, None]) * scale
    xi = jnp.exp((idx + 1)[None, :] * log_g[:, None])
    gdec = jnp.exp(c * log_g)

    def rows_by_head_lanes(t):
        return jnp.repeat(t.T[:, :, None], HEAD_DIM, axis=2).reshape(c, RET_HEADS * HEAD_DIM)

    return (dmat, rows_by_head_lanes(xi), rows_by_head_lanes(zeta),
            jnp.repeat(gdec, HEAD_DIM)[None, :])


def _resident(arr):
    nd = arr.ndim
    return pl.BlockSpec(arr.shape, lambda i: (0,) * nd, pipeline_mode=pl.Buffered(1))


_IN_HBM = pl.BlockSpec(memory_space=pl.ANY)


@jax.jit
def kernel(x, norm_in_g, w_in, conv_w, conv_b, gate_a_w, gate_a_b, gate_x_w, gate_x_b,
           lru_lambda, w_out, norm_out_g):
    batch, seq_len, d_model = x.shape
    assert d_model == D_MODEL and seq_len % STEP_BLOCK == 0
    assert w_in.shape == (D_MODEL, IN_WIDTH) and w_out.shape == (MIX_WIDTH, D_MODEL)
    steps_per_row = seq_len // STEP_BLOCK
    n_steps = batch * steps_per_row
    halves_per_row = 2 * steps_per_row

    cos_t, sin_t = _position_tables(seq_len)
    dmat, xi_t, zeta_t, gdec_t = _decay_tables()
    w_gates = jnp.concatenate([gate_a_w, gate_x_w], axis=-1).astype(BF16)
    row = lambda p: p.reshape(1, D_MODEL).astype(F32)

    def next_half(i):
        n = jnp.minimum(2 * (i + 1), 2 * n_steps - 2)
        return (n // halves_per_row, n % halves_per_row, 0)

    step_rows = pl.BlockSpec((1, STEP_BLOCK, D_MODEL),
                             lambda i: (i // steps_per_row, i % steps_per_row, 0))
    step_table = pl.BlockSpec((STEP_BLOCK, HEAD_DIM), lambda i: (i % steps_per_row, 0))
    small = (row(norm_in_g), conv_w.astype(F32), row(conv_b), w_gates, row(gate_a_b),
             row(gate_x_b), row(lru_lambda), row(norm_out_g), dmat, xi_t, zeta_t, gdec_t)
    (gin, convw, convb, wg, ba, bx, lam, gout, dmat, xi_t, zeta_t, gdec_t) = small
    operands_and_specs = (
        (x, step_rows), (x, pl.BlockSpec((1, HALF_BLOCK, D_MODEL), next_half)),
        (cos_t, step_table), (sin_t, step_table), (gin, _resident(gin)),
        (w_in.astype(F32), _IN_HBM), (convw, _resident(convw)), (convb, _resident(convb)),
        (wg, _resident(wg)), (ba, _resident(ba)), (bx, _resident(bx)), (lam, _resident(lam)),
        (w_out.astype(F32), _IN_HBM), (gout, _resident(gout)), (dmat, _resident(dmat)),
        (xi_t, _resident(xi_t)), (zeta_t, _resident(zeta_t)), (gdec_t, _resident(gdec_t)),
    )
    operands = [op for op, _ in operands_and_specs]
    in_specs = [spec for _, spec in operands_and_specs]

    return pl.pallas_call(
        functools.partial(_layer_body, steps_per_row=steps_per_row),
        grid=(n_steps,),
        in_specs=in_specs,
        out_specs=pl.BlockSpec((1, STEP_BLOCK, D_MODEL),
                               lambda i: (i // steps_per_row, i % steps_per_row, 0)),
        out_shape=jax.ShapeDtypeStruct(x.shape, x.dtype),
        scratch_shapes=[
            pltpu.VMEM((D_MODEL, IN_WIDTH), BF16),
            pltpu.VMEM((MIX_WIDTH, D_MODEL), BF16),
            pltpu.VMEM((2, WIN_STAGE_ROWS, IN_WIDTH), F32),
            pltpu.VMEM((2, WOUT_STAGE_ROWS, D_MODEL), F32),
            pltpu.SemaphoreType.DMA((2,)),
            pltpu.SemaphoreType.DMA((2,)),
            pltpu.VMEM((HALF_BLOCK, IN_WIDTH), F32),
            pltpu.VMEM((HALF_BLOCK, IN_WIDTH), F32),
            pltpu.VMEM((2, HALF_BLOCK, D_MODEL), BF16),
            pltpu.VMEM((2, HALF_BLOCK, D_MODEL), F32),
            pltpu.VMEM((RET_HEADS, HEAD_DIM, HEAD_DIM), F32),
            pltpu.VMEM((1, D_MODEL), F32),
            pltpu.VMEM((V7X_SUBLANES, D_MODEL), F32),
            pltpu.VMEM((2, HALF_BLOCK, MIX_WIDTH), BF16),
        ],
        compiler_params=pltpu.CompilerParams(
            dimension_semantics=("arbitrary",),
            vmem_limit_bytes=V7X_VMEM_LIMIT_BYTES,
        ),
        name="hybrid_layer",
    )(*operands)
```

```python
import functools

import jax
import jax.numpy as jnp
from jax import lax
from jax.experimental import pallas as pl
from jax.experimental.pallas import tpu as pltpu

D_MODEL = 1024
RET_HEADS = 8
HEAD_DIM = 128
RET_CHUNK = 128
ROPE_BASE = 10000.0
LRU_BLOCKS = 8
LRU_BLOCK_DIM = 128
LRU_C = 8.0
CONV_WIDTH = 4
MIX_WIDTH = 2 * D_MODEL
IN_WIDTH = 6 * D_MODEL
NORM_EPS = 1e-6

V7X_SUBLANES = 8
HALF_BLOCK = 256
STEP_BLOCK = 2 * HALF_BLOCK
MXU_PIECE_COLS = 512
LRU_ROWS = RET_CHUNK
WIN_STAGE_ROWS = 64
WOUT_STAGE_ROWS = 256
V7X_VMEM_LIMIT_BYTES = 58 * 1024 * 1024

_W_IN_PIECE, _W_OUT_PIECE, _W_LRU_UNIT, _W_RET_UNIT = 512.0, 1024.0, 320.0, 200.0

F32 = jnp.float32
BF16 = jnp.bfloat16
LOG2_E = 1.4426950408889634


def _sigmoid(z):
    return 1.0 / (1.0 + jnp.exp2(z * (-LOG2_E)))


def _rotate_half_apply(xh, cos_t, sin_t):
    return xh * cos_t + pltpu.roll(xh, HEAD_DIM // 2, 1) * sin_t


def _linear_scan(a, b, h0):
    rows = a.shape[0]
    groups = rows // V7X_SUBLANES
    a3 = a.reshape(groups, V7X_SUBLANES, a.shape[1])
    b3 = b.reshape(groups, V7X_SUBLANES, b.shape[1])
    row = lax.broadcasted_iota(jnp.int32, a3.shape, 1)
    shift = 1
    while shift < V7X_SUBLANES:
        keep = row >= shift
        a_prev = jnp.where(keep, pltpu.roll(a3, shift, 1), 1.0)
        b_prev = jnp.where(keep, pltpu.roll(b3, shift, 1), 0.0)
        b3 = a3 * b_prev + b3
        a3 = a3 * a_prev
        shift *= 2
    carry = h0
    out = []
    for g in range(groups):
        hg = b3[g] + a3[g] * carry
        out.append(hg)
        carry = hg[V7X_SUBLANES - 1:V7X_SUBLANES, :]
    return jnp.concatenate(out, axis=0)


def _load_weight_as_bf16(w_hbm, w_scr, stage, sems):
    chunk_rows = stage.shape[1]
    n_chunks = w_hbm.shape[0] // chunk_rows

    def chunk_copy(c, slot):
        return pltpu.make_async_copy(w_hbm.at[pl.ds(c * chunk_rows, chunk_rows), :],
                                     stage.at[slot], sems.at[slot])

    chunk_copy(0, 0).start()
    for c in range(n_chunks):
        slot = c % 2
        if c + 1 < n_chunks:
            chunk_copy(c + 1, 1 - slot).start()
        chunk_copy(c, slot).wait()
        w_scr[c * chunk_rows:(c + 1) * chunk_rows, :] = stage[slot].astype(BF16)


def _merge_by_weight(vector_units, matrix_units):
    total_v = sum(w for w, _ in vector_units) or 1.0
    total_m = sum(w for w, _ in matrix_units) or 1.0
    order, i, j, done_v, done_m = [], 0, 0, 0.0, 0.0
    while i < len(vector_units) or j < len(matrix_units):
        if j < len(matrix_units) and (i >= len(vector_units) or done_m / total_m <= done_v / total_v):
            w, f = matrix_units[j]
            j, done_m = j + 1, done_m + w
        else:
            w, f = vector_units[i]
            i, done_v = i + 1, done_v + w
        order.append(f)
    return order


def _layer_body(x_ref, xnext_ref, cos_ref, sin_ref, gin_ref, win_hbm, convw_ref, convb_ref,
                wg_ref, ba_ref, bx_ref, lam_ref, wout_hbm, gout_ref, dmat_ref, xi_ref,
                zeta_ref, gdec_ref, out_ref, win_scr, wout_scr, win_stage, wout_stage,
                win_sems, wout_sems, proj_a, proj_b, xn_scr, hres_scr, state_scr,
                hcarry_scr, tail_scr, mixed_scr, xstash_scr, out_stage, *, steps_per_row, n_steps):
    hb = HALF_BLOCK
    n_chunks = hb // RET_CHUNK
    step = pl.program_id(0)

    def norm_input(x, slot):
        ms = jnp.mean(x * x, axis=-1, keepdims=True)
        xn_scr[slot] = (x * lax.rsqrt(ms + NORM_EPS) * gin_ref[...]).astype(BF16)

    def in_piece(slot, proj_ref, p):
        cols = slice(p * MXU_PIECE_COLS, (p + 1) * MXU_PIECE_COLS)
        proj_ref[:, cols] = jnp.dot(xn_scr[slot], win_scr[:, cols],
                                    preferred_element_type=F32)

    def in_pieces(slot, proj_ref):
        return [(_W_IN_PIECE, functools.partial(in_piece, slot, proj_ref, p))
                for p in range(IN_WIDTH // MXU_PIECE_COLS)]

    def out_piece(half, p):
        cols = slice(p * MXU_PIECE_COLS, (p + 1) * MXU_PIECE_COLS)
        resid = x_ref[0, 0:hb, cols] if half == 0 else xstash_scr[:, cols]
        hres_scr[half, :, cols] = resid + jnp.dot(
            mixed_scr[half], wout_scr[:, cols], preferred_element_type=F32)

    def norm_output(half):
        hres = hres_scr[half]
        ms = jnp.mean(hres * hres, axis=-1, keepdims=True)
        y = hres * lax.rsqrt(ms + NORM_EPS) * gout_ref[...]
        if half == 0:
            out_stage[...] = y
        else:
            out_ref[0, hb:2 * hb, :] = y

    def out_pieces(half):
        n = D_MODEL // MXU_PIECE_COLS
        units = [(_W_OUT_PIECE, functools.partial(out_piece, half, p)) for p in range(n - 1)]

        def last():
            out_piece(half, n - 1)
            norm_output(half)

        return units + [(_W_OUT_PIECE, last)]

    def col(proj_ref, j, rows, lanes):
        return proj_ref[rows, slice(j * D_MODEL + lanes.start, j * D_MODEL + lanes.stop)]

    def retention_unit(proj_ref, half, h, c):
        sl = slice(h * HEAD_DIM, (h + 1) * HEAD_DIM)
        rs = slice(c * RET_CHUNK, (c + 1) * RET_CHUNK)
        trows = slice(half * hb + c * RET_CHUNK, half * hb + (c + 1) * RET_CHUNK)
        cos_t, sin_t = cos_ref[trows, :], sin_ref[trows, :]
        qc = _rotate_half_apply(col(proj_ref, 0, rs, sl), cos_t, sin_t)
        kc = _rotate_half_apply(col(proj_ref, 1, rs, sl), cos_t, sin_t)
        vc = col(proj_ref, 2, rs, sl).astype(BF16)
        state = state_scr[h]
        scores = lax.dot_general(qc.astype(BF16), kc.astype(BF16),
                                 (((1,), (1,)), ((), ())), preferred_element_type=F32)
        lhs = jnp.concatenate([(scores * dmat_ref[h]).astype(BF16),
                               (qc * xi_ref[:, sl]).astype(BF16)], axis=1)
        rhs = jnp.concatenate([vc, state.astype(BF16)], axis=0)
        o = jnp.dot(lhs, rhs, preferred_element_type=F32)
        kz = (kc * zeta_ref[:, sl]).astype(BF16)
        kv = lax.dot_general(kz, vc, (((0,), (0,)), ((), ())), preferred_element_type=F32)
        state_scr[h] = gdec_ref[:, sl] * state + kv
        mu = jnp.mean(o, axis=-1, keepdims=True)
        oc = o - mu
        var = jnp.mean(oc * oc, axis=-1, keepdims=True)
        gate = col(proj_ref, 3, rs, sl)
        y = oc * lax.rsqrt(var + NORM_EPS) * (gate * _sigmoid(gate))
        mixed_scr[half, rs, sl] = y.astype(BF16)

    def lru_unit(proj_ref, half, n, rh):
        sl = slice(n * LRU_BLOCK_DIM, (n + 1) * LRU_BLOCK_DIM)
        rows = slice(rh * LRU_ROWS, (rh + 1) * LRU_ROWS)
        groups = LRU_ROWS // V7X_SUBLANES
        x_lru = col(proj_ref, 4, rows, sl)
        if rh == 0:
            prev = tail_scr[:, sl]
        else:
            prev = col(proj_ref, 4, slice(rows.start - V7X_SUBLANES, rows.start), sl)
        if rh == hb // LRU_ROWS - 1:
            tail_scr[:, sl] = x_lru[LRU_ROWS - V7X_SUBLANES:LRU_ROWS]
        x3 = x_lru.reshape(groups, V7X_SUBLANES, LRU_BLOCK_DIM)
        all3 = jnp.concatenate([prev[None], x3], axis=0)
        sub = lax.broadcasted_iota(jnp.int32, x3.shape, 1)
        xc3 = convb_ref[:, sl] + x3 * convw_ref[CONV_WIDTH - 1:CONV_WIDTH, sl]
        for s in range(1, CONV_WIDTH):
            rolled = pltpu.roll(all3, s, 1)
            shifted = jnp.where(sub >= s, rolled[1:], rolled[:-1])
            xc3 = xc3 + shifted * convw_ref[CONV_WIDTH - 1 - s:CONV_WIDTH - s, sl]
        xc = xc3.reshape(LRU_ROWS, LRU_BLOCK_DIM)
        pre = jnp.dot(xc.astype(BF16), wg_ref[n], preferred_element_type=F32)
        z = -lam_ref[:, sl]
        softplus = jnp.maximum(z, 0.0) + jnp.log1p(jnp.exp(-jnp.abs(z)))
        r = _sigmoid(pre[:, :LRU_BLOCK_DIM] + ba_ref[:, sl])
        i = _sigmoid(pre[:, LRU_BLOCK_DIM:] + bx_ref[:, sl])
        a = jnp.exp2(r * ((-LRU_C * LOG2_E) * softplus))
        v = 1.0 - a * a
        mult = jnp.where(v > 0.0, v * lax.rsqrt(v), 0.0)
        b = mult * (i * xc)
        hseq = _linear_scan(a, b, hcarry_scr[:, sl])
        hcarry_scr[:, sl] = hseq[LRU_ROWS - 1:LRU_ROWS]
        gate = col(proj_ref, 5, rows, sl)
        osl = slice(D_MODEL + n * LRU_BLOCK_DIM, D_MODEL + (n + 1) * LRU_BLOCK_DIM)
        mixed_scr[half, rows, osl] = (hseq * (gate * _sigmoid(gate))).astype(BF16)

    def mixer_units(proj_ref, half):
        units = []
        for n in range(LRU_BLOCKS):
            for c in range(n_chunks):
                units.append((_W_LRU_UNIT, functools.partial(lru_unit, proj_ref, half, n, c)))
                units.append((_W_RET_UNIT, functools.partial(retention_unit, proj_ref, half, n, c)))
        return units

    @pl.when(step < n_steps)
    def _():
        @pl.when(step == 0)
        def _():
            _load_weight_as_bf16(win_hbm, win_scr, win_stage, win_sems)
            _load_weight_as_bf16(wout_hbm, wout_scr, wout_stage, wout_sems)
            norm_input(x_ref[0, 0:hb, :], 0)
            for _, piece in in_pieces(0, proj_a):
                piece()
            mixed_scr[1] = jnp.zeros(mixed_scr.shape[1:], BF16)
            xstash_scr[...] = jnp.zeros_like(xstash_scr)
            out_stage[...] = jnp.zeros_like(out_stage)

        @pl.when(step % steps_per_row == 0)
        def _():
            state_scr[...] = jnp.zeros_like(state_scr)
            hcarry_scr[...] = jnp.zeros_like(hcarry_scr)
            tail_scr[...] = jnp.zeros_like(tail_scr)

        out_ref[0, 0:hb, :] = out_stage[...]
        norm_input(x_ref[0, hb:2 * hb, :], 1)
        for thunk in _merge_by_weight(mixer_units(proj_a, 0), out_pieces(1) + in_pieces(1, proj_b)):
            thunk()
        xstash_scr[...] = x_ref[0, hb:2 * hb, :]
        norm_input(xnext_ref[0], 0)
        for thunk in _merge_by_weight(mixer_units(proj_b, 1), out_pieces(0) + in_pieces(0, proj_a)):
            thunk()

    @pl.when(step == n_steps)
    def _():
        out_ref[0, 0:hb, :] = out_stage[...]
        for _, piece in out_pieces(1):
            piece()


def _position_tables(seq_len):
    half = jnp.arange(0, HEAD_DIM, 2, dtype=F32)
    inv_freq = ROPE_BASE ** (-half / HEAD_DIM)
    ang = jnp.arange(seq_len, dtype=F32)[:, None] * inv_freq[None, :]
    cos, sin = jnp.cos(ang), jnp.sin(ang)
    cos_t = jnp.concatenate([cos, cos], axis=-1)
    sin_t = jnp.concatenate([-sin, sin], axis=-1)
    return cos_t, sin_t


def _decay_tables():
    c = RET_CHUNK
    scale = HEAD_DIM ** -0.5
    log_g = jnp.log1p(-jnp.exp2(-5.0 - jnp.arange(RET_HEADS, dtype=F32)))
    idx = jnp.arange(c, dtype=F32)
    rel = idx[:, None] - idx[None, :]
    dmat = jnp.where(rel[None] >= 0,
                     jnp.exp(jnp.maximum(rel, 0.0)[None] * log_g[:, None, None]), 0.0) * scale
    zeta = jnp.exp((c - 1 - idx)[None, :] * log_g[:, None]) * scale
    xi = jnp.exp((idx + 1)[None, :] * log_g[:, None])
    gdec = jnp.exp(c * log_g)

    def rows_by_head_lanes(t):
        return jnp.repeat(t.T[:, :, None], HEAD_DIM, axis=2).reshape(c, RET_HEADS * HEAD_DIM)

    return (dmat, rows_by_head_lanes(xi), rows_by_head_lanes(zeta),
            jnp.repeat(gdec, HEAD_DIM)[None, :])


def _resident(arr):
    nd = arr.ndim
    return pl.BlockSpec(arr.shape, lambda i: (0,) * nd, pipeline_mode=pl.Buffered(1))


_IN_HBM = pl.BlockSpec(memory_space=pl.ANY)


@jax.jit
def kernel(x, norm_in_g, w_in, conv_w, conv_b, gate_a_w, gate_a_b, gate_x_w, gate_x_b,
           lru_lambda, w_out, norm_out_g):
    batch, seq_len, d_model = x.shape
    assert d_model == D_MODEL and seq_len % STEP_BLOCK == 0
    assert w_in.shape == (D_MODEL, IN_WIDTH) and w_out.shape == (MIX_WIDTH, D_MODEL)
    steps_per_row = seq_len // STEP_BLOCK
    n_steps = batch * steps_per_row
    halves_per_row = 2 * steps_per_row

    cos_t, sin_t = _position_tables(seq_len)
    dmat, xi_t, zeta_t, gdec_t = _decay_tables()
    w_gates = jnp.concatenate([gate_a_w, gate_x_w], axis=-1).astype(BF16)
    row = lambda p: p.reshape(1, D_MODEL).astype(F32)

    def next_half(i):
        n = jnp.minimum(2 * (i + 1), 2 * n_steps - 2)
        return (n // halves_per_row, n % halves_per_row, 0)

    def block_of_step(i):
        j = jnp.minimum(i, n_steps - 1)
        return (j // steps_per_row, j % steps_per_row, 0)

    def block_of_previous_step(i):
        j = jnp.maximum(i - 1, 0)
        return (j // steps_per_row, j % steps_per_row, 0)

    step_rows = pl.BlockSpec((1, STEP_BLOCK, D_MODEL), block_of_step)
    step_table = pl.BlockSpec((STEP_BLOCK, HEAD_DIM), lambda i: block_of_step(i)[1:])
    small = (row(norm_in_g), conv_w.astype(F32), row(conv_b), w_gates, row(gate_a_b),
             row(gate_x_b), row(lru_lambda), row(norm_out_g), dmat, xi_t, zeta_t, gdec_t)
    (gin, convw, convb, wg, ba, bx, lam, gout, dmat, xi_t, zeta_t, gdec_t) = small
    operands_and_specs = (
        (x, step_rows), (x, pl.BlockSpec((1, HALF_BLOCK, D_MODEL), next_half)),
        (cos_t, step_table), (sin_t, step_table), (gin, _resident(gin)),
        (w_in.astype(F32), _IN_HBM), (convw, _resident(convw)), (convb, _resident(convb)),
        (wg, _resident(wg)), (ba, _resident(ba)), (bx, _resident(bx)), (lam, _resident(lam)),
        (w_out.astype(F32), _IN_HBM), (gout, _resident(gout)), (dmat, _resident(dmat)),
        (xi_t, _resident(xi_t)), (zeta_t, _resident(zeta_t)), (gdec_t, _resident(gdec_t)),
    )
    operands = [op for op, _ in operands_and_specs]
    in_specs = [spec for _, spec in operands_and_specs]

    return pl.pallas_call(
        functools.partial(_layer_body, steps_per_row=steps_per_row, n_steps=n_steps),
        grid=(n_steps + 1,),
        in_specs=in_specs,
        out_specs=pl.BlockSpec((1, STEP_BLOCK, D_MODEL), block_of_previous_step),
        out_shape=jax.ShapeDtypeStruct(x.shape, x.dtype),
        scratch_shapes=[
            pltpu.VMEM((D_MODEL, IN_WIDTH), BF16),
            pltpu.VMEM((MIX_WIDTH, D_MODEL), BF16),
            pltpu.VMEM((2, WIN_STAGE_ROWS, IN_WIDTH), F32),
            pltpu.VMEM((2, WOUT_STAGE_ROWS, D_MODEL), F32),
            pltpu.SemaphoreType.DMA((2,)),
            pltpu.SemaphoreType.DMA((2,)),
            pltpu.VMEM((HALF_BLOCK, IN_WIDTH), F32),
            pltpu.VMEM((HALF_BLOCK, IN_WIDTH), F32),
            pltpu.VMEM((2, HALF_BLOCK, D_MODEL), BF16),
            pltpu.VMEM((2, HALF_BLOCK, D_MODEL), F32),
            pltpu.VMEM((RET_HEADS, HEAD_DIM, HEAD_DIM), F32),
            pltpu.VMEM((1, D_MODEL), F32),
            pltpu.VMEM((V7X_SUBLANES, D_MODEL), F32),
            pltpu.VMEM((2, HALF_BLOCK, MIX_WIDTH), BF16),
            pltpu.VMEM((HALF_BLOCK, D_MODEL), F32),
            pltpu.VMEM((HALF_BLOCK, D_MODEL), F32),
        ],
        compiler_params=pltpu.CompilerParams(
            dimension_semantics=("arbitrary",),
            vmem_limit_bytes=V7X_VMEM_LIMIT_BYTES,
        ),
        name="hybrid_layer",
    )(*operands)
```

```python
import functools

import jax
import jax.numpy as jnp
from jax import lax
from jax.experimental import pallas as pl
from jax.experimental.pallas import tpu as pltpu

D_MODEL = 1024
RET_HEADS = 8
HEAD_DIM = 128
RET_CHUNK = 128
ROPE_BASE = 10000.0
LRU_BLOCKS = 8
LRU_BLOCK_DIM = 128
LRU_C = 8.0
CONV_WIDTH = 4
MIX_WIDTH = 2 * D_MODEL
IN_WIDTH = 6 * D_MODEL
NORM_EPS = 1e-6

V7X_SUBLANES = 8
HALF_BLOCK = 256
STEP_BLOCK = 2 * HALF_BLOCK
CHUNKS_PER_HALF = HALF_BLOCK // RET_CHUNK
MXU_PIECE_COLS = 512
HEADS_PER_PIECE = MXU_PIECE_COLS // HEAD_DIM
PIECES_PER_GROUP = D_MODEL // MXU_PIECE_COLS
LRU_ROWS = RET_CHUNK
WIN_STAGE_ROWS = 16
WOUT_STAGE_ROWS = 64
V7X_VMEM_LIMIT_BYTES = 60 * 1024 * 1024

Q_GROUP, K_GROUP, V_GROUP, GRET_GROUP, XLRU_GROUP, GLRU_GROUP = range(6)

F32 = jnp.float32
BF16 = jnp.bfloat16
LOG2_E = 1.4426950408889634


def _sigmoid(z):
    return 1.0 / (1.0 + jnp.exp2(z * (-LOG2_E)))


def _rotate_half_apply(xh, cos_t, sin_t):
    return xh * cos_t + pltpu.roll(xh, HEAD_DIM // 2, 1) * sin_t


def _linear_scan(a, b, h0):
    rows = a.shape[0]
    groups = rows // V7X_SUBLANES
    a3 = a.reshape(groups, V7X_SUBLANES, a.shape[1])
    b3 = b.reshape(groups, V7X_SUBLANES, b.shape[1])
    row = lax.broadcasted_iota(jnp.int32, a3.shape, 1)
    shift = 1
    while shift < V7X_SUBLANES:
        keep = row >= shift
        a_prev = jnp.where(keep, pltpu.roll(a3, shift, 1), 1.0)
        b_prev = jnp.where(keep, pltpu.roll(b3, shift, 1), 0.0)
        b3 = a3 * b_prev + b3
        a3 = a3 * a_prev
        shift *= 2
    carry = h0
    out = []
    for g in range(groups):
        hg = b3[g] + a3[g] * carry
        out.append(hg)
        carry = hg[V7X_SUBLANES - 1:V7X_SUBLANES, :]
    return jnp.concatenate(out, axis=0)


def _load_weight_as_bf16(w_hbm, w_scr, stage, sems):
    chunk_rows = stage.shape[1]
    n_chunks = w_hbm.shape[0] // chunk_rows

    def chunk_copy(c, slot):
        return pltpu.make_async_copy(w_hbm.at[pl.ds(c * chunk_rows, chunk_rows), :],
                                     stage.at[slot], sems.at[slot])

    chunk_copy(0, 0).start()

    def body(c, _):
        slot = c % 2

        @pl.when(c + 1 < n_chunks)
        def _():
            chunk_copy(c + 1, 1 - slot).start()

        chunk_copy(c, slot).wait()
        rows = pl.ds(pl.multiple_of(c * chunk_rows, chunk_rows), chunk_rows)
        w_scr[rows, :] = stage[slot].astype(BF16)
        return 0

    lax.fori_loop(0, n_chunks, body, 0)


def _interleave(vector_units, matrix_units):
    order, i, j = [], 0, 0
    nv, nm = max(len(vector_units), 1), max(len(matrix_units), 1)
    while i < len(vector_units) or j < len(matrix_units):
        if j < len(matrix_units) and (i >= len(vector_units) or j * nv <= i * nm):
            order.append(matrix_units[j])
            j += 1
        else:
            order.append(vector_units[i])
            i += 1
    return order


def _layer_body(x_ref, xnext_ref, cos_ref, sin_ref, cosn_ref, sinn_ref, gin_ref, win_hbm,
                convw_ref, convb_ref, wg_ref, ba_ref, bx_ref, lam_ref, wout_hbm, gout_ref,
                kscale_ref, xi_ref, gdec_ref, out_ref,
                win_scr, wout_scr, win_stage, wout_stage, win_sems, wout_sems,
                q_scr, kt_scr, v_scr, sd_scr, kv_scr, gret_scr, xc_scr, xcb_scr, glru_scr,
                xn_scr, hres_scr, state_scr, hcarry_scr, tail_scr, mixed_scr, xstash_scr,
                out_stage, *, steps_per_row, n_steps):
    hb = HALF_BLOCK
    step = pl.program_id(0)
    causal = (lax.broadcasted_iota(jnp.int32, (RET_CHUNK, RET_CHUNK), 0)
              >= lax.broadcasted_iota(jnp.int32, (RET_CHUNK, RET_CHUNK), 1))

    def head_lanes(h):
        return slice(h * HEAD_DIM, (h + 1) * HEAD_DIM)

    def chunk_rows(c):
        return slice(c * RET_CHUNK, (c + 1) * RET_CHUNK)

    def norm_input(x, slot):
        ms = jnp.mean(x * x, axis=-1, keepdims=True)
        xn_scr[slot] = (x * lax.rsqrt(ms + NORM_EPS) * gin_ref[...]).astype(BF16)

    def in_piece(slot, p, cos_t, sin_t, tail_is_zero):
        group, part = divmod(p, PIECES_PER_GROUP)
        cols = slice(p * MXU_PIECE_COLS, (p + 1) * MXU_PIECE_COLS)
        lanes = slice(part * MXU_PIECE_COLS, (part + 1) * MXU_PIECE_COLS)
        val = jnp.dot(xn_scr[slot], win_scr[:, cols], preferred_element_type=F32)
        if group == Q_GROUP:
            for j in range(HEADS_PER_PIECE):
                hl = head_lanes(j)
                gl = head_lanes(part * HEADS_PER_PIECE + j)
                q_scr[slot, :, gl] = _rotate_half_apply(val[:, hl], cos_t, sin_t).astype(BF16)
        elif group == K_GROUP:
            for j in range(HEADS_PER_PIECE):
                hl = head_lanes(j)
                gl = head_lanes(part * HEADS_PER_PIECE + j)
                kr = _rotate_half_apply(val[:, hl], cos_t, sin_t)
                for c in range(CHUNKS_PER_HALF):
                    kc = kr[chunk_rows(c)] * kscale_ref[:, gl]
                    kt_scr[gl, chunk_rows(c)] = kc.T.astype(BF16)
        elif group == V_GROUP:
            v_scr[slot, :, lanes] = val.astype(BF16)
        elif group == GRET_GROUP:
            gret_scr[slot, :, lanes] = val
        elif group == GLRU_GROUP:
            glru_scr[slot, :, lanes] = val
        else:
            groups = hb // V7X_SUBLANES
            prev = tail_scr[:, lanes]
            if tail_is_zero is not None:
                prev = jnp.where(tail_is_zero, 0.0, prev)
            tail_scr[:, lanes] = val[hb - V7X_SUBLANES:hb]
            x3 = val.reshape(groups, V7X_SUBLANES, MXU_PIECE_COLS)
            all3 = jnp.concatenate([prev[None], x3], axis=0)
            sub = lax.broadcasted_iota(jnp.int32, x3.shape, 1)
            xc3 = convb_ref[:, lanes] + x3 * convw_ref[CONV_WIDTH - 1:CONV_WIDTH, lanes]
            for s in range(1, CONV_WIDTH):
                rolled = pltpu.roll(all3, s, 1)
                shifted = jnp.where(sub >= s, rolled[1:], rolled[:-1])
                xc3 = xc3 + shifted * convw_ref[CONV_WIDTH - 1 - s:CONV_WIDTH - s, lanes]
            xc = xc3.reshape(hb, MXU_PIECE_COLS)
            xc_scr[slot, :, lanes] = xc
            xcb_scr[slot, :, lanes] = xc.astype(BF16)

    def retention_products(slot, h, c):
        hl, rs = head_lanes(h), chunk_rows(c)
        kt = kt_scr[hl, rs]
        scores = jnp.dot(q_scr[slot, rs, hl], kt, preferred_element_type=F32)
        sd_scr[slot, rs, hl] = jnp.where(causal, scores, 0.0).astype(BF16)
        kv_scr[slot, c, h] = jnp.dot(kt, v_scr[slot, rs, hl], preferred_element_type=F32)

    def projection_stage(slot, cos_t, sin_t, tail_is_zero):
        piece = lambda p: functools.partial(in_piece, slot, p, cos_t, sin_t, tail_is_zero)
        prods = lambda part: [functools.partial(retention_products, slot, h, c)
                              for h in range(part * HEADS_PER_PIECE, (part + 1) * HEADS_PER_PIECE)
                              for c in range(CHUNKS_PER_HALF)]
        order = []
        for part in range(PIECES_PER_GROUP):
            order += [piece(g * PIECES_PER_GROUP + part) for g in (Q_GROUP, K_GROUP, V_GROUP)]
            order += prods(part)
        order += [piece(g * PIECES_PER_GROUP + part) for g in (GRET_GROUP, XLRU_GROUP, GLRU_GROUP)
                  for part in range(PIECES_PER_GROUP)]
        return order

    def retention_unit(slot, h, c):
        hl, rs = head_lanes(h), chunk_rows(c)
        state = state_scr[h]
        lhs = jnp.concatenate([sd_scr[slot, rs, hl], q_scr[slot, rs, hl]], axis=1)
        rhs = jnp.concatenate([v_scr[slot, rs, hl], state.astype(BF16)], axis=0)
        o = jnp.dot(lhs, rhs, preferred_element_type=F32) * xi_ref[:, hl]
        state_scr[h] = gdec_ref[:, hl] * (state + kv_scr[slot, c, h])
        mu = jnp.mean(o, axis=-1, keepdims=True)
        oc = o - mu
        var = jnp.mean(oc * oc, axis=-1, keepdims=True)
        gate = gret_scr[slot, rs, hl]
        y = oc * lax.rsqrt(var + NORM_EPS) * (gate * _sigmoid(gate))
        mixed_scr[slot, rs, hl] = y.astype(BF16)

    def lru_unit(slot, n, rh):
        sl = slice(n * LRU_BLOCK_DIM, (n + 1) * LRU_BLOCK_DIM)
        rows = slice(rh * LRU_ROWS, (rh + 1) * LRU_ROWS)
        xc = xc_scr[slot, rows, sl]
        pre = jnp.dot(xcb_scr[slot, rows, sl], wg_ref[n], preferred_element_type=F32)
        z = -lam_ref[:, sl]
        softplus = jnp.maximum(z, 0.0) + jnp.log1p(jnp.exp(-jnp.abs(z)))
        r = _sigmoid(pre[:, :LRU_BLOCK_DIM] + ba_ref[:, sl])
        i = _sigmoid(pre[:, LRU_BLOCK_DIM:] + bx_ref[:, sl])
        a = jnp.exp2(r * ((-LRU_C * LOG2_E) * softplus))
        v = 1.0 - a * a
        mult = jnp.where(v > 0.0, v * lax.rsqrt(v), 0.0)
        b = mult * (i * xc)
        hseq = _linear_scan(a, b, hcarry_scr[:, sl])
        hcarry_scr[:, sl] = hseq[LRU_ROWS - 1:LRU_ROWS]
        gate = glru_scr[slot, rows, sl]
        osl = slice(D_MODEL + n * LRU_BLOCK_DIM, D_MODEL + (n + 1) * LRU_BLOCK_DIM)
        mixed_scr[slot, rows, osl] = (hseq * (gate * _sigmoid(gate))).astype(BF16)

    def mixer_stage(slot):
        units = []
        for n in range(LRU_BLOCKS):
            for c in range(CHUNKS_PER_HALF):
                units.append(functools.partial(lru_unit, slot, n, c))
                units.append(functools.partial(retention_unit, slot, n, c))
        return units

    def out_piece(half, p):
        cols = slice(p * MXU_PIECE_COLS, (p + 1) * MXU_PIECE_COLS)
        resid = x_ref[0, 0:hb, cols] if half == 0 else xstash_scr[:, cols]
        hres_scr[:, cols] = resid + jnp.dot(mixed_scr[half], wout_scr[:, cols],
                                            preferred_element_type=F32)

    def norm_output(half):
        hres = hres_scr[...]
        ms = jnp.mean(hres * hres, axis=-1, keepdims=True)
        y = hres * lax.rsqrt(ms + NORM_EPS) * gout_ref[...]
        if half == 0:
            out_stage[...] = y
        else:
            out_ref[0, hb:2 * hb, :] = y

    def output_stage(half):
        return [functools.partial(out_piece, half, p) for p in range(PIECES_PER_GROUP)] + [
            functools.partial(norm_output, half)]

    @pl.when(step < n_steps)
    def _():
        @pl.when(step == 0)
        def _():
            _load_weight_as_bf16(win_hbm, win_scr, win_stage, win_sems)
            _load_weight_as_bf16(wout_hbm, wout_scr, wout_stage, wout_sems)
            norm_input(x_ref[0, 0:hb, :], 0)
            tail_scr[...] = jnp.zeros_like(tail_scr)
            for thunk in projection_stage(0, cos_ref[0:hb, :], sin_ref[0:hb, :], None):
                thunk()
            mixed_scr[1] = jnp.zeros(mixed_scr.shape[1:], BF16)
            xstash_scr[...] = jnp.zeros_like(xstash_scr)
            out_stage[...] = jnp.zeros_like(out_stage)

        @pl.when(step % steps_per_row == 0)
        def _():
            state_scr[...] = jnp.zeros_like(state_scr)
            hcarry_scr[...] = jnp.zeros_like(hcarry_scr)

        out_ref[0, 0:hb, :] = out_stage[...]
        norm_input(x_ref[0, hb:2 * hb, :], 1)
        matrix = output_stage(1) + projection_stage(1, cos_ref[hb:2 * hb, :], sin_ref[hb:2 * hb, :], None)
        for thunk in _interleave(mixer_stage(0), matrix):
            thunk()
        xstash_scr[...] = x_ref[0, hb:2 * hb, :]
        norm_input(xnext_ref[0], 0)
        next_starts_row = (step + 1) % steps_per_row == 0
        matrix = output_stage(0) + projection_stage(0, cosn_ref[...], sinn_ref[...], next_starts_row)
        for thunk in _interleave(mixer_stage(1), matrix):
            thunk()

    @pl.when(step == n_steps)
    def _():
        out_ref[0, 0:hb, :] = out_stage[...]
        for thunk in output_stage(1):
            thunk()


def _position_tables(seq_len):
    half = jnp.arange(0, HEAD_DIM, 2, dtype=F32)
    inv_freq = ROPE_BASE ** (-half / HEAD_DIM)
    ang = jnp.arange(seq_len, dtype=F32)[:, None] * inv_freq[None, :]
    cos, sin = jnp.cos(ang), jnp.sin(ang)
    cos_t = jnp.concatenate([cos, cos], axis=-1)
    sin_t = jnp.concatenate([-sin, sin], axis=-1)
    return cos_t, sin_t


def _decay_tables():
    c = RET_CHUNK
    scale = HEAD_DIM ** -0.5
    log_g = jnp.log1p(-jnp.exp2(-5.0 - jnp.arange(RET_HEADS, dtype=F32)))
    idx = jnp.arange(c, dtype=F32)
    xi = jnp.exp((idx + 1)[None, :] * log_g[:, None])
    kscale = scale * jnp.exp(-(idx + 1)[None, :] * log_g[:, None])
    gdec = jnp.exp(c * log_g)

    def rows_by_head_lanes(t):
        return jnp.repeat(t.T[:, :, None], HEAD_DIM, axis=2).reshape(c, RET_HEADS * HEAD_DIM)

    return rows_by_head_lanes(kscale), rows_by_head_lanes(xi), jnp.repeat(gdec, HEAD_DIM)[None, :]


def _resident(arr):
    nd = arr.ndim
    return pl.BlockSpec(arr.shape, lambda i: (0,) * nd, pipeline_mode=pl.Buffered(1))


_IN_HBM = pl.BlockSpec(memory_space=pl.ANY)


@jax.jit
def kernel(x, norm_in_g, w_in, conv_w, conv_b, gate_a_w, gate_a_b, gate_x_w, gate_x_b,
           lru_lambda, w_out, norm_out_g):
    batch, seq_len, d_model = x.shape
    assert d_model == D_MODEL and seq_len % STEP_BLOCK == 0
    assert w_in.shape == (D_MODEL, IN_WIDTH) and w_out.shape == (MIX_WIDTH, D_MODEL)
    steps_per_row = seq_len // STEP_BLOCK
    n_steps = batch * steps_per_row
    halves_per_row = 2 * steps_per_row

    cos_t, sin_t = _position_tables(seq_len)
    kscale_t, xi_t, gdec_t = _decay_tables()
    w_gates = jnp.concatenate([gate_a_w, gate_x_w], axis=-1).astype(BF16)
    row = lambda p: p.reshape(1, D_MODEL).astype(F32)

    def next_half(i):
        n = jnp.minimum(2 * (i + 1), 2 * n_steps - 2)
        return (n // halves_per_row, n % halves_per_row, 0)

    def block_of_step(i):
        j = jnp.minimum(i, n_steps - 1)
        return (j // steps_per_row, j % steps_per_row, 0)

    def block_of_previous_step(i):
        j = jnp.maximum(i - 1, 0)
        return (j // steps_per_row, j % steps_per_row, 0)

    step_rows = pl.BlockSpec((1, STEP_BLOCK, D_MODEL), block_of_step)
    step_table = pl.BlockSpec((STEP_BLOCK, HEAD_DIM), lambda i: block_of_step(i)[1:])
    next_table = pl.BlockSpec((HALF_BLOCK, HEAD_DIM), lambda i: next_half(i)[1:])
    small = (row(norm_in_g), conv_w.astype(F32), row(conv_b), w_gates, row(gate_a_b),
             row(gate_x_b), row(lru_lambda), row(norm_out_g), kscale_t, xi_t, gdec_t)
    (gin, convw, convb, wg, ba, bx, lam, gout, kscale_t, xi_t, gdec_t) = small
    operands_and_specs = (
        (x, step_rows), (x, pl.BlockSpec((1, HALF_BLOCK, D_MODEL), next_half)),
        (cos_t, step_table), (sin_t, step_table), (cos_t, next_table), (sin_t, next_table),
        (gin, _resident(gin)), (w_in.astype(F32), _IN_HBM), (convw, _resident(convw)),
        (convb, _resident(convb)), (wg, _resident(wg)), (ba, _resident(ba)), (bx, _resident(bx)),
        (lam, _resident(lam)), (w_out.astype(F32), _IN_HBM), (gout, _resident(gout)),
        (kscale_t, _resident(kscale_t)), (xi_t, _resident(xi_t)), (gdec_t, _resident(gdec_t)),
    )
    operands = [op for op, _ in operands_and_specs]
    in_specs = [spec for _, spec in operands_and_specs]

    half_f32 = pltpu.VMEM((2, HALF_BLOCK, D_MODEL), F32)
    half_bf16 = pltpu.VMEM((2, HALF_BLOCK, D_MODEL), BF16)
    return pl.pallas_call(
        functools.partial(_layer_body, steps_per_row=steps_per_row, n_steps=n_steps),
        grid=(n_steps + 1,),
        in_specs=in_specs,
        out_specs=pl.BlockSpec((1, STEP_BLOCK, D_MODEL), block_of_previous_step),
        out_shape=jax.ShapeDtypeStruct(x.shape, x.dtype),
        scratch_shapes=[
            pltpu.VMEM((D_MODEL, IN_WIDTH), BF16),
            pltpu.VMEM((MIX_WIDTH, D_MODEL), BF16),
            pltpu.VMEM((2, WIN_STAGE_ROWS, IN_WIDTH), F32),
            pltpu.VMEM((2, WOUT_STAGE_ROWS, D_MODEL), F32),
            pltpu.SemaphoreType.DMA((2,)),
            pltpu.SemaphoreType.DMA((2,)),
            half_bf16,
            pltpu.VMEM((D_MODEL, HALF_BLOCK), BF16),
            half_bf16,
            half_bf16,
            pltpu.VMEM((2, CHUNKS_PER_HALF, RET_HEADS, HEAD_DIM, HEAD_DIM), F32),
            half_f32,
            half_f32,
            half_bf16,
            half_f32,
            half_bf16,
            pltpu.VMEM((HALF_BLOCK, D_MODEL), F32),
            pltpu.VMEM((RET_HEADS, HEAD_DIM, HEAD_DIM), F32),
            pltpu.VMEM((1, D_MODEL), F32),
            pltpu.VMEM((V7X_SUBLANES, D_MODEL), F32),
            pltpu.VMEM((2, HALF_BLOCK, MIX_WIDTH), BF16),
            pltpu.VMEM((HALF_BLOCK, D_MODEL), F32),
            pltpu.VMEM((HALF_BLOCK, D_MODEL), F32),
        ],
        compiler_params=pltpu.CompilerParams(
            dimension_semantics=("arbitrary",),
            vmem_limit_bytes=V7X_VMEM_LIMIT_BYTES,
        ),
        name="hybrid_layer",
    )(*operands)
```

```python
import functools

import jax
import jax.numpy as jnp
import numpy as np
from jax import lax
from jax.experimental import pallas as pl
from jax.experimental.pallas import tpu as pltpu

D_MODEL = 1024
RET_HEADS = 8
HEAD_DIM = 128
RET_CHUNK = 128
ROPE_BASE = 10000.0
LRU_BLOCKS = 8
LRU_BLOCK_DIM = 128
LRU_C = 8.0
CONV_WIDTH = 4
MIX_WIDTH = 2 * D_MODEL
IN_WIDTH = 6 * D_MODEL
NORM_EPS = 1e-6

V7X_SUBLANES = 8
HALF_BLOCK = 256
STEP_BLOCK = 2 * HALF_BLOCK
CHUNKS_PER_HALF = HALF_BLOCK // RET_CHUNK
MXU_PIECE_COLS = 512
HEADS_PER_PIECE = MXU_PIECE_COLS // HEAD_DIM
PIECES_PER_GROUP = D_MODEL // MXU_PIECE_COLS
LRU_ROWS = RET_CHUNK
WEIGHT_BLOCK_ROWS = HALF_BLOCK
WEIGHT_STAGES = 6
V7X_VMEM_LIMIT_BYTES = 60 * 1024 * 1024

Q_GROUP, K_GROUP, V_GROUP, GRET_GROUP, XLRU_GROUP, GLRU_GROUP = range(6)

F32 = jnp.float32
BF16 = jnp.bfloat16
LOG2_E = 1.4426950408889634


def _sigmoid(z):
    return 1.0 / (1.0 + jnp.exp2(z * (-LOG2_E)))


def _rotate_half_apply(xh, cos_t, sin_t):
    return xh * cos_t + pltpu.roll(xh, HEAD_DIM // 2, 1) * sin_t


def _linear_scan(a, b, h0):
    rows = a.shape[0]
    groups = rows // V7X_SUBLANES
    a3 = a.reshape(groups, V7X_SUBLANES, a.shape[1])
    b3 = b.reshape(groups, V7X_SUBLANES, b.shape[1])
    row = lax.broadcasted_iota(jnp.int32, a3.shape, 1)
    shift = 1
    while shift < V7X_SUBLANES:
        keep = row >= shift
        a_prev = jnp.where(keep, pltpu.roll(a3, shift, 1), 1.0)
        b_prev = jnp.where(keep, pltpu.roll(b3, shift, 1), 0.0)
        b3 = a3 * b_prev + b3
        a3 = a3 * a_prev
        shift *= 2
    carry = h0
    out = []
    for g in range(groups):
        hg = b3[g] + a3[g] * carry
        out.append(hg)
        carry = hg[V7X_SUBLANES - 1:V7X_SUBLANES, :]
    return jnp.concatenate(out, axis=0)


def _load_weights_as_bf16(weights, stages, sems):
    blocks = [(w_hbm, w_scr, r, c)
              for w_hbm, w_scr in weights
              for r in range(w_hbm.shape[0] // WEIGHT_BLOCK_ROWS)
              for c in range(w_hbm.shape[1] // D_MODEL)]
    n_stage = len(stages)

    def block_copy(k):
        w_hbm, _, r, c = blocks[k]
        src = w_hbm.at[pl.ds(r * WEIGHT_BLOCK_ROWS, WEIGHT_BLOCK_ROWS), pl.ds(c * D_MODEL, D_MODEL)]
        return pltpu.make_async_copy(src, stages[k % n_stage], sems.at[k % n_stage])

    for k in range(min(n_stage, len(blocks))):
        block_copy(k).start()
    for k, (_, w_scr, r, c) in enumerate(blocks):
        block_copy(k).wait()
        w_scr[r * WEIGHT_BLOCK_ROWS:(r + 1) * WEIGHT_BLOCK_ROWS, c * D_MODEL:(c + 1) * D_MODEL] = (
            stages[k % n_stage][...].astype(BF16))
        if k + n_stage < len(blocks):
            block_copy(k + n_stage).start()


def _interleave(vector_units, matrix_units):
    order, i, j = [], 0, 0
    nv, nm = max(len(vector_units), 1), max(len(matrix_units), 1)
    while i < len(vector_units) or j < len(matrix_units):
        if j < len(matrix_units) and (i >= len(vector_units) or j * nv <= i * nm):
            order.append(matrix_units[j])
            j += 1
        else:
            order.append(vector_units[i])
            i += 1
    return order


def _layer_body(x_ref, xnext_ref, cos_ref, sin_ref, cosn_ref, sinn_ref, gin_ref, win_hbm,
                convw_ref, convb_ref, wg_ref, ba_ref, bx_ref, lam_ref, wout_hbm, gout_ref,
                kscale_ref, xi_ref, gdec_ref, out_ref,
                win_scr, wout_scr, weight_sems,
                q_scr, kt_scr, v_scr, sd_scr, kv_scr, gret_scr, xc_scr, xcb_scr, glru_scr,
                xn_scr, hres_scr, state_scr, hcarry_scr, tail_scr, mixed_scr, xstash_scr,
                out_stage, *, steps_per_row, n_steps):
    hb = HALF_BLOCK
    step = pl.program_id(0)
    causal = (lax.broadcasted_iota(jnp.int32, (RET_CHUNK, RET_CHUNK), 0)
              >= lax.broadcasted_iota(jnp.int32, (RET_CHUNK, RET_CHUNK), 1))

    def head_lanes(h):
        return slice(h * HEAD_DIM, (h + 1) * HEAD_DIM)

    def chunk_rows(c):
        return slice(c * RET_CHUNK, (c + 1) * RET_CHUNK)

    def norm_input(x, slot):
        ms = jnp.mean(x * x, axis=-1, keepdims=True)
        xn_scr[slot] = (x * lax.rsqrt(ms + NORM_EPS) * gin_ref[...]).astype(BF16)

    def in_piece(slot, p, cos_t, sin_t, tail_is_zero):
        group, part = divmod(p, PIECES_PER_GROUP)
        cols = slice(p * MXU_PIECE_COLS, (p + 1) * MXU_PIECE_COLS)
        lanes = slice(part * MXU_PIECE_COLS, (part + 1) * MXU_PIECE_COLS)
        val = jnp.dot(xn_scr[slot], win_scr[:, cols], preferred_element_type=F32)
        if group == Q_GROUP:
            for j in range(HEADS_PER_PIECE):
                hl = head_lanes(j)
                gl = head_lanes(part * HEADS_PER_PIECE + j)
                q_scr[slot, :, gl] = _rotate_half_apply(val[:, hl], cos_t, sin_t).astype(BF16)
        elif group == K_GROUP:
            for j in range(HEADS_PER_PIECE):
                hl = head_lanes(j)
                gl = head_lanes(part * HEADS_PER_PIECE + j)
                kr = _rotate_half_apply(val[:, hl], cos_t, sin_t)
                for c in range(CHUNKS_PER_HALF):
                    kc = kr[chunk_rows(c)] * kscale_ref[:, gl]
                    kt_scr[gl, chunk_rows(c)] = kc.T.astype(BF16)
        elif group == V_GROUP:
            v_scr[slot, :, lanes] = val.astype(BF16)
        elif group == GRET_GROUP:
            gret_scr[slot, :, lanes] = val
        elif group == GLRU_GROUP:
            glru_scr[slot, :, lanes] = val
        else:
            groups = hb // V7X_SUBLANES
            prev = tail_scr[:, lanes]
            if tail_is_zero is not None:
                prev = jnp.where(tail_is_zero, 0.0, prev)
            tail_scr[:, lanes] = val[hb - V7X_SUBLANES:hb]
            x3 = val.reshape(groups, V7X_SUBLANES, MXU_PIECE_COLS)
            all3 = jnp.concatenate([prev[None], x3], axis=0)
            sub = lax.broadcasted_iota(jnp.int32, x3.shape, 1)
            xc3 = convb_ref[:, lanes] + x3 * convw_ref[CONV_WIDTH - 1:CONV_WIDTH, lanes]
            for s in range(1, CONV_WIDTH):
                rolled = pltpu.roll(all3, s, 1)
                shifted = jnp.where(sub >= s, rolled[1:], rolled[:-1])
                xc3 = xc3 + shifted * convw_ref[CONV_WIDTH - 1 - s:CONV_WIDTH - s, lanes]
            xc = xc3.reshape(hb, MXU_PIECE_COLS)
            xc_scr[slot, :, lanes] = xc
            xcb_scr[slot, :, lanes] = xc.astype(BF16)

    def retention_products(slot, h, c):
        hl, rs = head_lanes(h), chunk_rows(c)
        kt = kt_scr[hl, rs]
        scores = jnp.dot(q_scr[slot, rs, hl], kt, preferred_element_type=F32)
        sd_scr[slot, rs, hl] = jnp.where(causal, scores, 0.0).astype(BF16)
        kv_scr[slot, c, h] = jnp.dot(kt, v_scr[slot, rs, hl], preferred_element_type=F32)

    def projection_stage(slot, cos_t, sin_t, tail_is_zero):
        piece = lambda p: functools.partial(in_piece, slot, p, cos_t, sin_t, tail_is_zero)
        prods = lambda part: [functools.partial(retention_products, slot, h, c)
                              for h in range(part * HEADS_PER_PIECE, (part + 1) * HEADS_PER_PIECE)
                              for c in range(CHUNKS_PER_HALF)]
        order = []
        for part in range(PIECES_PER_GROUP):
            order += [piece(g * PIECES_PER_GROUP + part) for g in (Q_GROUP, K_GROUP, V_GROUP)]
            order += prods(part)
        order += [piece(g * PIECES_PER_GROUP + part) for g in (GRET_GROUP, XLRU_GROUP, GLRU_GROUP)
                  for part in range(PIECES_PER_GROUP)]
        return order

    def retention_unit(slot, h, c):
        hl, rs = head_lanes(h), chunk_rows(c)
        state = state_scr[h]
        lhs = jnp.concatenate([sd_scr[slot, rs, hl], q_scr[slot, rs, hl]], axis=1)
        rhs = jnp.concatenate([v_scr[slot, rs, hl], state.astype(BF16)], axis=0)
        o = jnp.dot(lhs, rhs, preferred_element_type=F32) * xi_ref[:, hl]
        state_scr[h] = gdec_ref[:, hl] * (state + kv_scr[slot, c, h])
        mu = jnp.mean(o, axis=-1, keepdims=True)
        oc = o - mu
        var = jnp.mean(oc * oc, axis=-1, keepdims=True)
        gate = gret_scr[slot, rs, hl]
        y = oc * lax.rsqrt(var + NORM_EPS) * (gate * _sigmoid(gate))
        mixed_scr[slot, rs, hl] = y.astype(BF16)

    def lru_unit(slot, n, rh):
        sl = slice(n * LRU_BLOCK_DIM, (n + 1) * LRU_BLOCK_DIM)
        rows = slice(rh * LRU_ROWS, (rh + 1) * LRU_ROWS)
        xc = xc_scr[slot, rows, sl]
        pre = jnp.dot(xcb_scr[slot, rows, sl], wg_ref[n], preferred_element_type=F32)
        z = -lam_ref[:, sl]
        softplus = jnp.maximum(z, 0.0) + jnp.log1p(jnp.exp(-jnp.abs(z)))
        r = _sigmoid(pre[:, :LRU_BLOCK_DIM] + ba_ref[:, sl])
        i = _sigmoid(pre[:, LRU_BLOCK_DIM:] + bx_ref[:, sl])
        a = jnp.exp2(r * ((-LRU_C * LOG2_E) * softplus))
        v = 1.0 - a * a
        mult = jnp.where(v > 0.0, v * lax.rsqrt(v), 0.0)
        b = mult * (i * xc)
        hseq = _linear_scan(a, b, hcarry_scr[:, sl])
        hcarry_scr[:, sl] = hseq[LRU_ROWS - 1:LRU_ROWS]
        gate = glru_scr[slot, rows, sl]
        osl = slice(D_MODEL + n * LRU_BLOCK_DIM, D_MODEL + (n + 1) * LRU_BLOCK_DIM)
        mixed_scr[slot, rows, osl] = (hseq * (gate * _sigmoid(gate))).astype(BF16)

    def mixer_stage(slot):
        units = []
        for n in range(LRU_BLOCKS):
            for c in range(CHUNKS_PER_HALF):
                units.append(functools.partial(lru_unit, slot, n, c))
                units.append(functools.partial(retention_unit, slot, n, c))
        return units

    def out_piece(half, p):
        cols = slice(p * MXU_PIECE_COLS, (p + 1) * MXU_PIECE_COLS)
        resid = x_ref[0, 0:hb, cols] if half == 0 else xstash_scr[:, cols]
        hres_scr[:, cols] = resid + jnp.dot(mixed_scr[half], wout_scr[:, cols],
                                            preferred_element_type=F32)

    def norm_output(half):
        hres = hres_scr[...]
        ms = jnp.mean(hres * hres, axis=-1, keepdims=True)
        y = hres * lax.rsqrt(ms + NORM_EPS) * gout_ref[...]
        if half == 0:
            out_stage[...] = y
        else:
            out_ref[0, hb:2 * hb, :] = y

    def output_stage(half):
        return [functools.partial(out_piece, half, p) for p in range(PIECES_PER_GROUP)] + [
            functools.partial(norm_output, half)]

    @pl.when(step < n_steps)
    def _():
        @pl.when(step == 0)
        def _():
            stages = [buf.at[slot] for buf in (gret_scr, xc_scr, glru_scr) for slot in range(2)]
            _load_weights_as_bf16([(win_hbm, win_scr), (wout_hbm, wout_scr)], stages, weight_sems)
            norm_input(x_ref[0, 0:hb, :], 0)
            tail_scr[...] = jnp.zeros_like(tail_scr)
            for thunk in projection_stage(0, cos_ref[0:hb, :], sin_ref[0:hb, :], None):
                thunk()
            mixed_scr[1] = jnp.zeros(mixed_scr.shape[1:], BF16)
            xstash_scr[...] = jnp.zeros_like(xstash_scr)
            out_stage[...] = jnp.zeros_like(out_stage)

        @pl.when(step % steps_per_row == 0)
        def _():
            state_scr[...] = jnp.zeros_like(state_scr)
            hcarry_scr[...] = jnp.zeros_like(hcarry_scr)

        out_ref[0, 0:hb, :] = out_stage[...]
        norm_input(x_ref[0, hb:2 * hb, :], 1)
        matrix = output_stage(1) + projection_stage(1, cos_ref[hb:2 * hb, :], sin_ref[hb:2 * hb, :], None)
        for thunk in _interleave(mixer_stage(0), matrix):
            thunk()
        xstash_scr[...] = x_ref[0, hb:2 * hb, :]
        norm_input(xnext_ref[0], 0)
        next_starts_row = (step + 1) % steps_per_row == 0
        matrix = output_stage(0) + projection_stage(0, cosn_ref[...], sinn_ref[...], next_starts_row)
        for thunk in _interleave(mixer_stage(1), matrix):
            thunk()

    @pl.when(step == n_steps)
    def _():
        out_ref[0, 0:hb, :] = out_stage[...]
        for thunk in output_stage(1):
            thunk()


def _position_tables(seq_len):
    half = np.arange(0, HEAD_DIM, 2, dtype=np.float64)
    inv_freq = ROPE_BASE ** (-half / HEAD_DIM)
    ang = np.arange(seq_len, dtype=np.float64)[:, None] * inv_freq[None, :]
    cos, sin = np.cos(ang), np.sin(ang)
    cos_t = np.concatenate([cos, cos], axis=-1).astype(np.float32)
    sin_t = np.concatenate([-sin, sin], axis=-1).astype(np.float32)
    return cos_t, sin_t


def _decay_tables():
    c = RET_CHUNK
    scale = HEAD_DIM ** -0.5
    log_g = np.log1p(-np.exp2(-5.0 - np.arange(RET_HEADS, dtype=np.float64)))
    idx = np.arange(c, dtype=np.float64)
    xi = np.exp((idx + 1)[None, :] * log_g[:, None])
    kscale = scale * np.exp(-(idx + 1)[None, :] * log_g[:, None])
    gdec = np.exp(c * log_g)

    def rows_by_head_lanes(t):
        return np.repeat(t.T[:, :, None], HEAD_DIM, axis=2).reshape(c, RET_HEADS * HEAD_DIM)

    f32 = lambda t: t.astype(np.float32)
    return (f32(rows_by_head_lanes(kscale)), f32(rows_by_head_lanes(xi)),
            f32(np.repeat(gdec, HEAD_DIM)[None, :]))


def _resident(arr):
    nd = arr.ndim
    return pl.BlockSpec(arr.shape, lambda i: (0,) * nd, pipeline_mode=pl.Buffered(1))


_IN_HBM = pl.BlockSpec(memory_space=pl.ANY)


@jax.jit
def kernel(x, norm_in_g, w_in, conv_w, conv_b, gate_a_w, gate_a_b, gate_x_w, gate_x_b,
           lru_lambda, w_out, norm_out_g):
    batch, seq_len, d_model = x.shape
    assert d_model == D_MODEL and seq_len % STEP_BLOCK == 0
    assert w_in.shape == (D_MODEL, IN_WIDTH) and w_out.shape == (MIX_WIDTH, D_MODEL)
    steps_per_row = seq_len // STEP_BLOCK
    n_steps = batch * steps_per_row
    halves_per_row = 2 * steps_per_row

    cos_t, sin_t = _position_tables(seq_len)
    kscale_t, xi_t, gdec_t = _decay_tables()
    w_gates = jnp.concatenate([gate_a_w, gate_x_w], axis=-1).astype(BF16)
    row = lambda p: p.reshape(1, D_MODEL).astype(F32)

    def next_half(i):
        n = jnp.minimum(2 * (i + 1), 2 * n_steps - 2)
        return (n // halves_per_row, n % halves_per_row, 0)

    def block_of_step(i):
        j = jnp.minimum(i, n_steps - 1)
        return (j // steps_per_row, j % steps_per_row, 0)

    def block_of_previous_step(i):
        j = jnp.maximum(i - 1, 0)
        return (j // steps_per_row, j % steps_per_row, 0)

    step_rows = pl.BlockSpec((1, STEP_BLOCK, D_MODEL), block_of_step)
    step_table = pl.BlockSpec((STEP_BLOCK, HEAD_DIM), lambda i: block_of_step(i)[1:])
    next_table = pl.BlockSpec((HALF_BLOCK, HEAD_DIM), lambda i: next_half(i)[1:])
    small = (row(norm_in_g), conv_w.astype(F32), row(conv_b), w_gates, row(gate_a_b),
             row(gate_x_b), row(lru_lambda), row(norm_out_g), kscale_t, xi_t, gdec_t)
    (gin, convw, convb, wg, ba, bx, lam, gout, kscale_t, xi_t, gdec_t) = small
    operands_and_specs = (
        (x, step_rows), (x, pl.BlockSpec((1, HALF_BLOCK, D_MODEL), next_half)),
        (cos_t, step_table), (sin_t, step_table), (cos_t, next_table), (sin_t, next_table),
        (gin, _resident(gin)), (w_in.astype(F32), _IN_HBM), (convw, _resident(convw)),
        (convb, _resident(convb)), (wg, _resident(wg)), (ba, _resident(ba)), (bx, _resident(bx)),
        (lam, _resident(lam)), (w_out.astype(F32), _IN_HBM), (gout, _resident(gout)),
        (kscale_t, _resident(kscale_t)), (xi_t, _resident(xi_t)), (gdec_t, _resident(gdec_t)),
    )
    operands = [op for op, _ in operands_and_specs]
    in_specs = [spec for _, spec in operands_and_specs]

    half_f32 = pltpu.VMEM((2, HALF_BLOCK, D_MODEL), F32)
    half_bf16 = pltpu.VMEM((2, HALF_BLOCK, D_MODEL), BF16)
    return pl.pallas_call(
        functools.partial(_layer_body, steps_per_row=steps_per_row, n_steps=n_steps),
        grid=(n_steps + 1,),
        in_specs=in_specs,
        out_specs=pl.BlockSpec((1, STEP_BLOCK, D_MODEL), block_of_previous_step),
        out_shape=jax.ShapeDtypeStruct(x.shape, x.dtype),
        scratch_shapes=[
            pltpu.VMEM((D_MODEL, IN_WIDTH), BF16),
            pltpu.VMEM((MIX_WIDTH, D_MODEL), BF16),
            pltpu.SemaphoreType.DMA((WEIGHT_STAGES,)),
            half_bf16,
            pltpu.VMEM((D_MODEL, HALF_BLOCK), BF16),
            half_bf16,
            half_bf16,
            pltpu.VMEM((2, CHUNKS_PER_HALF, RET_HEADS, HEAD_DIM, HEAD_DIM), F32),
            half_f32,
            half_f32,
            half_bf16,
            half_f32,
            half_bf16,
            pltpu.VMEM((HALF_BLOCK, D_MODEL), F32),
            pltpu.VMEM((RET_HEADS, HEAD_DIM, HEAD_DIM), F32),
            pltpu.VMEM((1, D_MODEL), F32),
            pltpu.VMEM((V7X_SUBLANES, D_MODEL), F32),
            pltpu.VMEM((2, HALF_BLOCK, MIX_WIDTH), BF16),
            pltpu.VMEM((HALF_BLOCK, D_MODEL), F32),
            pltpu.VMEM((HALF_BLOCK, D_MODEL), F32),
        ],
        compiler_params=pltpu.CompilerParams(
            dimension_semantics=("arbitrary",),
            vmem_limit_bytes=V7X_VMEM_LIMIT_BYTES,
        ),
        name="hybrid_layer",
    )(*operands)
```

```python
import functools

import jax
import jax.numpy as jnp
import numpy as np
from jax import lax
from jax.experimental import pallas as pl
from jax.experimental.pallas import tpu as pltpu

D_MODEL = 1024
RET_HEADS = 8
HEAD_DIM = 128
RET_CHUNK = 128
ROPE_BASE = 10000.0
LRU_BLOCKS = 8
LRU_BLOCK_DIM = 128
LRU_C = 8.0
CONV_WIDTH = 4
MIX_WIDTH = 2 * D_MODEL
IN_WIDTH = 6 * D_MODEL
NORM_EPS = 1e-6

V7X_SUBLANES = 8
HALF_BLOCK = 256
STEP_BLOCK = 2 * HALF_BLOCK
CHUNKS_PER_HALF = HALF_BLOCK // RET_CHUNK
MXU_PIECE_COLS = 512
HEADS_PER_PIECE = MXU_PIECE_COLS // HEAD_DIM
PIECES_PER_GROUP = D_MODEL // MXU_PIECE_COLS
LRU_ROWS = RET_CHUNK
RUN = V7X_SUBLANES
GROUP_ROWS = RUN * V7X_SUBLANES
GROUPS_PER_HALF = HALF_BLOCK // GROUP_ROWS
WEIGHT_BLOCK_ROWS = HALF_BLOCK
WEIGHT_STAGES = 6
V7X_VMEM_LIMIT_BYTES = 60 * 1024 * 1024

Q_GROUP, K_GROUP, V_GROUP, GRET_GROUP, XLRU_GROUP, GLRU_GROUP = range(6)

F32 = jnp.float32
BF16 = jnp.bfloat16
LOG2_E = 1.4426950408889634


def _sigmoid(z):
    return 1.0 / (1.0 + jnp.exp2(z * (-LOG2_E)))


def _rotate_half_apply(xh, cos_t, sin_t):
    return xh * cos_t + pltpu.roll(xh, HEAD_DIM // 2, 1) * sin_t


def _linear_scan_interleaved(a, b, h0):
    lanes = a.shape[1]
    n_groups = a.shape[0] // GROUP_ROWS
    a4 = a.reshape(n_groups, RUN, V7X_SUBLANES, lanes)
    b4 = b.reshape(n_groups, RUN, V7X_SUBLANES, lanes)
    sub = lax.broadcasted_iota(jnp.int32, (V7X_SUBLANES, lanes), 0)
    carry = jnp.broadcast_to(h0, (V7X_SUBLANES, lanes))
    out = []
    for g in range(n_groups):
        h_loc, a_cum = [b4[g, 0]], [a4[g, 0]]
        for i in range(1, RUN):
            h_loc.append(a4[g, i] * h_loc[-1] + b4[g, i])
            a_cum.append(a4[g, i] * a_cum[-1])
        p, e = a_cum[-1], h_loc[-1]
        shift = 1
        while shift < V7X_SUBLANES:
            keep = sub >= shift
            p_prev = jnp.where(keep, pltpu.roll(p, shift, 0), 1.0)
            e_prev = jnp.where(keep, pltpu.roll(e, shift, 0), 0.0)
            e = p * e_prev + e
            p = p * p_prev
            shift *= 2
        first = sub >= 1
        entry = (jnp.where(first, pltpu.roll(e, 1, 0), 0.0)
                 + jnp.where(first, pltpu.roll(p, 1, 0), 1.0) * carry)
        hs = [h_loc[i] + a_cum[i] * entry for i in range(RUN)]
        out += hs
        last = hs[-1][V7X_SUBLANES - 1:V7X_SUBLANES, :]
        carry = jnp.broadcast_to(last, (V7X_SUBLANES, lanes))
    return jnp.concatenate(out, axis=0), last


def _load_weights_as_bf16(weights, stages, sems):
    blocks = [(w_hbm, w_scr, r, c)
              for w_hbm, w_scr in weights
              for r in range(w_hbm.shape[0] // WEIGHT_BLOCK_ROWS)
              for c in range(w_hbm.shape[1] // D_MODEL)]
    n_stage = len(stages)

    def block_copy(k):
        w_hbm, _, r, c = blocks[k]
        src = w_hbm.at[pl.ds(r * WEIGHT_BLOCK_ROWS, WEIGHT_BLOCK_ROWS), pl.ds(c * D_MODEL, D_MODEL)]
        return pltpu.make_async_copy(src, stages[k % n_stage], sems.at[k % n_stage])

    for k in range(min(n_stage, len(blocks))):
        block_copy(k).start()
    for k, (_, w_scr, r, c) in enumerate(blocks):
        block_copy(k).wait()
        w_scr[r * WEIGHT_BLOCK_ROWS:(r + 1) * WEIGHT_BLOCK_ROWS, c * D_MODEL:(c + 1) * D_MODEL] = (
            stages[k % n_stage][...].astype(BF16))
        if k + n_stage < len(blocks):
            block_copy(k + n_stage).start()


def _interleave(vector_units, matrix_units):
    order, i, j = [], 0, 0
    nv, nm = max(len(vector_units), 1), max(len(matrix_units), 1)
    while i < len(vector_units) or j < len(matrix_units):
        if j < len(matrix_units) and (i >= len(vector_units) or j * nv <= i * nm):
            order.append(matrix_units[j])
            j += 1
        else:
            order.append(vector_units[i])
            i += 1
    return order


def _layer_body(x_ref, xnext_ref, cos_ref, sin_ref, cosn_ref, sinn_ref, gin_ref, win_hbm,
                convw_ref, convb_ref, wg_ref, ba_ref, bx_ref, lam_ref, wout_hbm, gout_ref,
                kscale_ref, xi_ref, gdec_ref, out_ref,
                win_scr, wout_scr, weight_sems,
                q_scr, kt_scr, v_scr, sd_scr, kv_scr, gret_scr, xc_scr, xcb_scr, glru_scr,
                xn_scr, xnp_scr, xslab_scr, yslab_scr, hres_scr, state_scr, hcarry_scr, tail_scr,
                mixed_scr, xstash_scr, out_stage, *, steps_per_row, n_steps):
    hb = HALF_BLOCK
    step = pl.program_id(0)
    causal = (lax.broadcasted_iota(jnp.int32, (RET_CHUNK, RET_CHUNK), 0)
              >= lax.broadcasted_iota(jnp.int32, (RET_CHUNK, RET_CHUNK), 1))

    def head_lanes(h):
        return slice(h * HEAD_DIM, (h + 1) * HEAD_DIM)

    def chunk_rows(c):
        return slice(c * RET_CHUNK, (c + 1) * RET_CHUNK)

    def norm_input(x, slot):
        ms = jnp.mean(x * x, axis=-1, keepdims=True)
        xn = x * lax.rsqrt(ms + NORM_EPS) * gin_ref[...]
        xn_scr[slot] = xn.astype(BF16)
        for nb in range(D_MODEL // LRU_BLOCK_DIM):
            xslab_scr[nb] = xn[:, nb * LRU_BLOCK_DIM:(nb + 1) * LRU_BLOCK_DIM]
        pieces = []
        for g in range(GROUPS_PER_HALF):
            for i in range(RUN):
                rows = pl.ds(g * GROUP_ROWS + i, V7X_SUBLANES, stride=RUN)
                pieces.append(jnp.concatenate(
                    [xslab_scr[nb, rows, :] for nb in range(D_MODEL // LRU_BLOCK_DIM)], axis=1))
        xnp_scr[slot] = jnp.concatenate(pieces, axis=0).astype(BF16)

    def in_piece(slot, p, cos_t, sin_t, tail_is_zero):
        group, part = divmod(p, PIECES_PER_GROUP)
        cols = slice(p * MXU_PIECE_COLS, (p + 1) * MXU_PIECE_COLS)
        lanes = slice(part * MXU_PIECE_COLS, (part + 1) * MXU_PIECE_COLS)
        lhs = xnp_scr[slot] if group in (XLRU_GROUP, GLRU_GROUP) else xn_scr[slot]
        val = jnp.dot(lhs, win_scr[:, cols], preferred_element_type=F32)
        if group == Q_GROUP:
            for j in range(HEADS_PER_PIECE):
                hl = head_lanes(j)
                gl = head_lanes(part * HEADS_PER_PIECE + j)
                q_scr[slot, :, gl] = _rotate_half_apply(val[:, hl], cos_t, sin_t).astype(BF16)
        elif group == K_GROUP:
            for j in range(HEADS_PER_PIECE):
                hl = head_lanes(j)
                gl = head_lanes(part * HEADS_PER_PIECE + j)
                kr = _rotate_half_apply(val[:, hl], cos_t, sin_t)
                for c in range(CHUNKS_PER_HALF):
                    kc = kr[chunk_rows(c)] * kscale_ref[:, gl]
                    kt_scr[gl, chunk_rows(c)] = kc.T.astype(BF16)
        elif group == V_GROUP:
            v_scr[slot, :, lanes] = val.astype(BF16)
        elif group == GRET_GROUP:
            gret_scr[slot, :, lanes] = val
        elif group == GLRU_GROUP:
            glru_scr[slot, :, lanes] = val
        else:
            taps = CONV_WIDTH - 1
            x4 = val.reshape(GROUPS_PER_HALF, RUN, V7X_SUBLANES, MXU_PIECE_COLS)
            prev_tail = tail_scr[:, :, lanes]
            if tail_is_zero is not None:
                prev_tail = jnp.where(tail_is_zero, 0.0, prev_tail)
            tail_scr[:, :, lanes] = x4[GROUPS_PER_HALF - 1, RUN - taps:RUN]
            late = x4[:, RUN - taps:RUN]
            late_prev = jnp.concatenate([prev_tail[None], late[:-1]], axis=0)
            flat = (GROUPS_PER_HALF * taps, V7X_SUBLANES, MXU_PIECE_COLS)
            sub = lax.broadcasted_iota(jnp.int32, flat, 1)
            wrapped = jnp.where(sub >= 1, pltpu.roll(late.reshape(flat), 1, 1),
                                pltpu.roll(late_prev.reshape(flat), 1, 1)).reshape(late.shape)
            xc4 = convb_ref[:, lanes] + x4 * convw_ref[taps:CONV_WIDTH, lanes]
            for s in range(1, CONV_WIDTH):
                src = jnp.concatenate([wrapped[:, taps - s:], x4[:, :RUN - s]], axis=1)
                xc4 = xc4 + src * convw_ref[taps - s:CONV_WIDTH - s, lanes]
            xc = xc4.reshape(hb, MXU_PIECE_COLS)
            xc_scr[slot, :, lanes] = xc
            xcb_scr[slot, :, lanes] = xc.astype(BF16)

    def retention_products(slot, h, c):
        hl, rs = head_lanes(h), chunk_rows(c)
        kt = kt_scr[hl, rs]
        scores = jnp.dot(q_scr[slot, rs, hl], kt, preferred_element_type=F32)
        sd_scr[slot, rs, hl] = jnp.where(causal, scores, 0.0).astype(BF16)
        kv_scr[slot, c, h] = jnp.dot(kt, v_scr[slot, rs, hl], preferred_element_type=F32)

    def projection_stage(slot, cos_t, sin_t, tail_is_zero):
        piece = lambda p: functools.partial(in_piece, slot, p, cos_t, sin_t, tail_is_zero)
        prods = lambda part: [functools.partial(retention_products, slot, h, c)
                              for h in range(part * HEADS_PER_PIECE, (part + 1) * HEADS_PER_PIECE)
                              for c in range(CHUNKS_PER_HALF)]
        order = []
        for part in range(PIECES_PER_GROUP):
            order += [piece(g * PIECES_PER_GROUP + part) for g in (Q_GROUP, K_GROUP, V_GROUP)]
            order += prods(part)
        order += [piece(g * PIECES_PER_GROUP + part) for g in (GRET_GROUP, XLRU_GROUP, GLRU_GROUP)
                  for part in range(PIECES_PER_GROUP)]
        return order

    def retention_unit(slot, h, c):
        hl, rs = head_lanes(h), chunk_rows(c)
        state = state_scr[h]
        lhs = jnp.concatenate([sd_scr[slot, rs, hl], q_scr[slot, rs, hl]], axis=1)
        rhs = jnp.concatenate([v_scr[slot, rs, hl], state.astype(BF16)], axis=0)
        o = jnp.dot(lhs, rhs, preferred_element_type=F32) * xi_ref[:, hl]
        state_scr[h] = gdec_ref[:, hl] * (state + kv_scr[slot, c, h])
        mu = jnp.mean(o, axis=-1, keepdims=True)
        oc = o - mu
        var = jnp.mean(oc * oc, axis=-1, keepdims=True)
        gate = gret_scr[slot, rs, hl]
        y = oc * lax.rsqrt(var + NORM_EPS) * (gate * _sigmoid(gate))
        mixed_scr[slot, rs, hl] = y.astype(BF16)

    def lru_unit(slot, n, rh):
        sl = slice(n * LRU_BLOCK_DIM, (n + 1) * LRU_BLOCK_DIM)
        rows = slice(rh * LRU_ROWS, (rh + 1) * LRU_ROWS)
        xc = xc_scr[slot, rows, sl]
        pre = jnp.dot(xcb_scr[slot, rows, sl], wg_ref[n], preferred_element_type=F32)
        z = -lam_ref[:, sl]
        softplus = jnp.maximum(z, 0.0) + jnp.log1p(jnp.exp(-jnp.abs(z)))
        r = _sigmoid(pre[:, :LRU_BLOCK_DIM] + ba_ref[:, sl])
        i = _sigmoid(pre[:, LRU_BLOCK_DIM:] + bx_ref[:, sl])
        a = jnp.exp2(r * ((-LRU_C * LOG2_E) * softplus))
        v = 1.0 - a * a
        mult = jnp.where(v > 0.0, v * lax.rsqrt(v), 0.0)
        b = mult * (i * xc)
        hseq, hcarry_scr[:, sl] = _linear_scan_interleaved(a, b, hcarry_scr[:, sl])
        gate = glru_scr[slot, rows, sl]
        y = hseq * (gate * _sigmoid(gate))
        for g in range(LRU_ROWS // GROUP_ROWS):
            for i in range(RUN):
                src = slice(g * GROUP_ROWS + i * V7X_SUBLANES, g * GROUP_ROWS + (i + 1) * V7X_SUBLANES)
                dst = pl.ds(rows.start + g * GROUP_ROWS + i, V7X_SUBLANES, stride=RUN)
                yslab_scr[slot, n, dst, :] = y[src]

    def pack_lru_outputs(half):
        for n in range(LRU_BLOCKS):
            osl = slice(D_MODEL + n * LRU_BLOCK_DIM, D_MODEL + (n + 1) * LRU_BLOCK_DIM)
            mixed_scr[half, :, osl] = yslab_scr[half, n].astype(BF16)

    def mixer_stage(slot):
        units = []
        for n in range(LRU_BLOCKS):
            for c in range(CHUNKS_PER_HALF):
                units.append(functools.partial(lru_unit, slot, n, c))
                units.append(functools.partial(retention_unit, slot, n, c))
        return units

    def out_piece(half, p):
        cols = slice(p * MXU_PIECE_COLS, (p + 1) * MXU_PIECE_COLS)
        resid = x_ref[0, 0:hb, cols] if half == 0 else xstash_scr[:, cols]
        hres_scr[:, cols] = resid + jnp.dot(mixed_scr[half], wout_scr[:, cols],
                                            preferred_element_type=F32)

    def norm_output(half):
        hres = hres_scr[...]
        ms = jnp.mean(hres * hres, axis=-1, keepdims=True)
        y = hres * lax.rsqrt(ms + NORM_EPS) * gout_ref[...]
        if half == 0:
            out_stage[...] = y
        else:
            out_ref[0, hb:2 * hb, :] = y

    def output_stage(half):
        return [functools.partial(pack_lru_outputs, half)] + [
            functools.partial(out_piece, half, p) for p in range(PIECES_PER_GROUP)] + [
            functools.partial(norm_output, half)]

    @pl.when(step < n_steps)
    def _():
        @pl.when(step == 0)
        def _():
            stages = [buf.at[slot] for buf in (gret_scr, xc_scr, glru_scr) for slot in range(2)]
            _load_weights_as_bf16([(win_hbm, win_scr), (wout_hbm, wout_scr)], stages, weight_sems)
            norm_input(x_ref[0, 0:hb, :], 0)
            tail_scr[...] = jnp.zeros_like(tail_scr)
            for thunk in projection_stage(0, cos_ref[0:hb, :], sin_ref[0:hb, :], None):
                thunk()
            mixed_scr[1] = jnp.zeros(mixed_scr.shape[1:], BF16)
            yslab_scr[1] = jnp.zeros(yslab_scr.shape[1:], F32)
            xstash_scr[...] = jnp.zeros_like(xstash_scr)
            out_stage[...] = jnp.zeros_like(out_stage)

        @pl.when(step % steps_per_row == 0)
        def _():
            state_scr[...] = jnp.zeros_like(state_scr)
            hcarry_scr[...] = jnp.zeros_like(hcarry_scr)

        out_ref[0, 0:hb, :] = out_stage[...]
        norm_input(x_ref[0, hb:2 * hb, :], 1)
        matrix = output_stage(1) + projection_stage(1, cos_ref[hb:2 * hb, :], sin_ref[hb:2 * hb, :], None)
        for thunk in _interleave(mixer_stage(0), matrix):
            thunk()
        xstash_scr[...] = x_ref[0, hb:2 * hb, :]
        norm_input(xnext_ref[0], 0)
        next_starts_row = (step + 1) % steps_per_row == 0
        matrix = output_stage(0) + projection_stage(0, cosn_ref[...], sinn_ref[...], next_starts_row)
        for thunk in _interleave(mixer_stage(1), matrix):
            thunk()

    @pl.when(step == n_steps)
    def _():
        out_ref[0, 0:hb, :] = out_stage[...]
        for thunk in output_stage(1):
            thunk()


def _position_tables(seq_len):
    half = np.arange(0, HEAD_DIM, 2, dtype=np.float64)
    inv_freq = ROPE_BASE ** (-half / HEAD_DIM)
    ang = np.arange(seq_len, dtype=np.float64)[:, None] * inv_freq[None, :]
    cos, sin = np.cos(ang), np.sin(ang)
    cos_t = np.concatenate([cos, cos], axis=-1).astype(np.float32)
    sin_t = np.concatenate([-sin, sin], axis=-1).astype(np.float32)
    return cos_t, sin_t


def _decay_tables():
    c = RET_CHUNK
    scale = HEAD_DIM ** -0.5
    log_g = np.log1p(-np.exp2(-5.0 - np.arange(RET_HEADS, dtype=np.float64)))
    idx = np.arange(c, dtype=np.float64)
    xi = np.exp((idx + 1)[None, :] * log_g[:, None])
    kscale = scale * np.exp(-(idx + 1)[None, :] * log_g[:, None])
    gdec = np.exp(c * log_g)

    def rows_by_head_lanes(t):
        return np.repeat(t.T[:, :, None], HEAD_DIM, axis=2).reshape(c, RET_HEADS * HEAD_DIM)

    f32 = lambda t: t.astype(np.float32)
    return (f32(rows_by_head_lanes(kscale)), f32(rows_by_head_lanes(xi)),
            f32(np.repeat(gdec, HEAD_DIM)[None, :]))


def _resident(arr):
    nd = arr.ndim
    return pl.BlockSpec(arr.shape, lambda i: (0,) * nd, pipeline_mode=pl.Buffered(1))


_IN_HBM = pl.BlockSpec(memory_space=pl.ANY)


@jax.jit
def kernel(x, norm_in_g, w_in, conv_w, conv_b, gate_a_w, gate_a_b, gate_x_w, gate_x_b,
           lru_lambda, w_out, norm_out_g):
    batch, seq_len, d_model = x.shape
    assert d_model == D_MODEL and seq_len % STEP_BLOCK == 0
    assert w_in.shape == (D_MODEL, IN_WIDTH) and w_out.shape == (MIX_WIDTH, D_MODEL)
    steps_per_row = seq_len // STEP_BLOCK
    n_steps = batch * steps_per_row
    halves_per_row = 2 * steps_per_row

    cos_t, sin_t = _position_tables(seq_len)
    kscale_t, xi_t, gdec_t = _decay_tables()
    w_gates = jnp.concatenate([gate_a_w, gate_x_w], axis=-1).astype(BF16)
    row = lambda p: p.reshape(1, D_MODEL).astype(F32)

    def next_half(i):
        n = jnp.minimum(2 * (i + 1), 2 * n_steps - 2)
        return (n // halves_per_row, n % halves_per_row, 0)

    def block_of_step(i):
        j = jnp.minimum(i, n_steps - 1)
        return (j // steps_per_row, j % steps_per_row, 0)

    def block_of_previous_step(i):
        j = jnp.maximum(i - 1, 0)
        return (j // steps_per_row, j % steps_per_row, 0)

    step_rows = pl.BlockSpec((1, STEP_BLOCK, D_MODEL), block_of_step)
    step_table = pl.BlockSpec((STEP_BLOCK, HEAD_DIM), lambda i: block_of_step(i)[1:])
    next_table = pl.BlockSpec((HALF_BLOCK, HEAD_DIM), lambda i: next_half(i)[1:])
    small = (row(norm_in_g), conv_w.astype(F32), row(conv_b), w_gates, row(gate_a_b),
             row(gate_x_b), row(lru_lambda), row(norm_out_g), kscale_t, xi_t, gdec_t)
    (gin, convw, convb, wg, ba, bx, lam, gout, kscale_t, xi_t, gdec_t) = small
    operands_and_specs = (
        (x, step_rows), (x, pl.BlockSpec((1, HALF_BLOCK, D_MODEL), next_half)),
        (cos_t, step_table), (sin_t, step_table), (cos_t, next_table), (sin_t, next_table),
        (gin, _resident(gin)), (w_in.astype(F32), _IN_HBM), (convw, _resident(convw)),
        (convb, _resident(convb)), (wg, _resident(wg)), (ba, _resident(ba)), (bx, _resident(bx)),
        (lam, _resident(lam)), (w_out.astype(F32), _IN_HBM), (gout, _resident(gout)),
        (kscale_t, _resident(kscale_t)), (xi_t, _resident(xi_t)), (gdec_t, _resident(gdec_t)),
    )
    operands = [op for op, _ in operands_and_specs]
    in_specs = [spec for _, spec in operands_and_specs]

    half_f32 = pltpu.VMEM((2, HALF_BLOCK, D_MODEL), F32)
    half_bf16 = pltpu.VMEM((2, HALF_BLOCK, D_MODEL), BF16)
    return pl.pallas_call(
        functools.partial(_layer_body, steps_per_row=steps_per_row, n_steps=n_steps),
        grid=(n_steps + 1,),
        in_specs=in_specs,
        out_specs=pl.BlockSpec((1, STEP_BLOCK, D_MODEL), block_of_previous_step),
        out_shape=jax.ShapeDtypeStruct(x.shape, x.dtype),
        scratch_shapes=[
            pltpu.VMEM((D_MODEL, IN_WIDTH), BF16),
            pltpu.VMEM((MIX_WIDTH, D_MODEL), BF16),
            pltpu.SemaphoreType.DMA((WEIGHT_STAGES,)),
            half_bf16,
            pltpu.VMEM((D_MODEL, HALF_BLOCK), BF16),
            half_bf16,
            half_bf16,
            pltpu.VMEM((2, CHUNKS_PER_HALF, RET_HEADS, HEAD_DIM, HEAD_DIM), F32),
            half_f32,
            half_f32,
            half_bf16,
            half_f32,
            half_bf16,
            half_bf16,
            pltpu.VMEM((D_MODEL // LRU_BLOCK_DIM, HALF_BLOCK, LRU_BLOCK_DIM), F32),
            pltpu.VMEM((2, LRU_BLOCKS, HALF_BLOCK, LRU_BLOCK_DIM), F32),
            pltpu.VMEM((HALF_BLOCK, D_MODEL), F32),
            pltpu.VMEM((RET_HEADS, HEAD_DIM, HEAD_DIM), F32),
            pltpu.VMEM((1, D_MODEL), F32),
            pltpu.VMEM((CONV_WIDTH - 1, V7X_SUBLANES, D_MODEL), F32),
            pltpu.VMEM((2, HALF_BLOCK, MIX_WIDTH), BF16),
            pltpu.VMEM((HALF_BLOCK, D_MODEL), F32),
            pltpu.VMEM((HALF_BLOCK, D_MODEL), F32),
        ],
        compiler_params=pltpu.CompilerParams(
            dimension_semantics=("arbitrary",),
            vmem_limit_bytes=V7X_VMEM_LIMIT_BYTES,
        ),
        name="hybrid_layer",
    )(*operands)
```

```python
import functools

import jax
import jax.numpy as jnp
import numpy as np
from jax import lax
from jax.experimental import pallas as pl
from jax.experimental.pallas import tpu as pltpu

D_MODEL = 1024
RET_HEADS = 8
HEAD_DIM = 128
RET_CHUNK = 128
ROPE_BASE = 10000.0
LRU_BLOCKS = 8
LRU_BLOCK_DIM = 128
LRU_C = 8.0
CONV_WIDTH = 4
MIX_WIDTH = 2 * D_MODEL
IN_WIDTH = 6 * D_MODEL
NORM_EPS = 1e-6

V7X_SUBLANES = 8
HALF_BLOCK = 256
STEP_BLOCK = 2 * HALF_BLOCK
CHUNKS_PER_HALF = HALF_BLOCK // RET_CHUNK
assert CHUNKS_PER_HALF == 2 and RET_HEADS % 2 == 0
MXU_PIECE_COLS = 512
HEADS_PER_PIECE = MXU_PIECE_COLS // HEAD_DIM
PIECES_PER_GROUP = D_MODEL // MXU_PIECE_COLS
LRU_ROWS = RET_CHUNK
RUN = V7X_SUBLANES
GROUP_ROWS = RUN * V7X_SUBLANES
GROUPS_PER_HALF = HALF_BLOCK // GROUP_ROWS
WEIGHT_BLOCK_ROWS = HALF_BLOCK
WEIGHT_STAGES = 6
V7X_VMEM_LIMIT_BYTES = 60 * 1024 * 1024

Q_GROUP, K_GROUP, V_GROUP, GRET_GROUP, XLRU_GROUP, GLRU_GROUP = range(6)

F32 = jnp.float32
BF16 = jnp.bfloat16
LOG2_E = 1.4426950408889634


def _sigmoid(z):
    return 1.0 / (1.0 + jnp.exp2(z * (-LOG2_E)))


def _rotate_half_apply(xh, cos_t, sin_t):
    return xh * cos_t + pltpu.roll(xh, HEAD_DIM // 2, 1) * sin_t


def _linear_scan_interleaved(a, b, h0):
    lanes = a.shape[1]
    n_groups = a.shape[0] // GROUP_ROWS
    a4 = a.reshape(n_groups, RUN, V7X_SUBLANES, lanes)
    b4 = b.reshape(n_groups, RUN, V7X_SUBLANES, lanes)
    sub = lax.broadcasted_iota(jnp.int32, (V7X_SUBLANES, lanes), 0)
    carry = jnp.broadcast_to(h0, (V7X_SUBLANES, lanes))
    out = []
    for g in range(n_groups):
        h_loc, a_cum = [b4[g, 0]], [a4[g, 0]]
        for i in range(1, RUN):
            h_loc.append(a4[g, i] * h_loc[-1] + b4[g, i])
            a_cum.append(a4[g, i] * a_cum[-1])
        p, e = a_cum[-1], h_loc[-1]
        shift = 1
        while shift < V7X_SUBLANES:
            keep = sub >= shift
            p_prev = jnp.where(keep, pltpu.roll(p, shift, 0), 1.0)
            e_prev = jnp.where(keep, pltpu.roll(e, shift, 0), 0.0)
            e = p * e_prev + e
            p = p * p_prev
            shift *= 2
        first = sub >= 1
        entry = (jnp.where(first, pltpu.roll(e, 1, 0), 0.0)
                 + jnp.where(first, pltpu.roll(p, 1, 0), 1.0) * carry)
        hs = [h_loc[i] + a_cum[i] * entry for i in range(RUN)]
        out += hs
        last = hs[-1][V7X_SUBLANES - 1:V7X_SUBLANES, :]
        carry = jnp.broadcast_to(last, (V7X_SUBLANES, lanes))
    return jnp.concatenate(out, axis=0), last


def _load_weights_as_bf16(weights, stages, sems):
    blocks = [(w_hbm, w_scr, r, c)
              for w_hbm, w_scr in weights
              for r in range(w_hbm.shape[0] // WEIGHT_BLOCK_ROWS)
              for c in range(w_hbm.shape[1] // D_MODEL)]
    n_stage = len(stages)

    def block_copy(k):
        w_hbm, _, r, c = blocks[k]
        src = w_hbm.at[pl.ds(r * WEIGHT_BLOCK_ROWS, WEIGHT_BLOCK_ROWS), pl.ds(c * D_MODEL, D_MODEL)]
        return pltpu.make_async_copy(src, stages[k % n_stage], sems.at[k % n_stage])

    for k in range(min(n_stage, len(blocks))):
        block_copy(k).start()
    for k, (_, w_scr, r, c) in enumerate(blocks):
        block_copy(k).wait()
        w_scr[r * WEIGHT_BLOCK_ROWS:(r + 1) * WEIGHT_BLOCK_ROWS, c * D_MODEL:(c + 1) * D_MODEL] = (
            stages[k % n_stage][...].astype(BF16))
        if k + n_stage < len(blocks):
            block_copy(k + n_stage).start()


def _interleave(vector_units, matrix_units):
    order, i, j = [], 0, 0
    nv, nm = max(len(vector_units), 1), max(len(matrix_units), 1)
    while i < len(vector_units) or j < len(matrix_units):
        if j < len(matrix_units) and (i >= len(vector_units) or j * nv <= i * nm):
            order.append(matrix_units[j])
            j += 1
        else:
            order.append(vector_units[i])
            i += 1
    return order


def _layer_body(x_ref, xnext_ref, cos_ref, sin_ref, cosn_ref, sinn_ref, gin_ref, win_hbm,
                convw_ref, convb_ref, wg_ref, ba_ref, bx_ref, lam_ref, wout_hbm, gout_ref,
                kscale_ref, xi_ref, gdec_ref, out_ref,
                win_scr, wout_scr, weight_sems,
                q_scr, kt_scr, v_scr, sd_scr, kv_scr, gret_scr, xc_scr, xcb_scr, glru_scr,
                xn_scr, xnp_scr, xslab_scr, yslab_scr, hres_scr, state_scr, hcarry_scr, tail_scr,
                mixed_scr, xstash_scr, out_stage, *, steps_per_row, n_steps):
    hb = HALF_BLOCK
    step = pl.program_id(0)
    causal = (lax.broadcasted_iota(jnp.int32, (RET_CHUNK, RET_CHUNK), 0)
              >= lax.broadcasted_iota(jnp.int32, (RET_CHUNK, RET_CHUNK), 1))

    def head_lanes(h):
        return slice(h * HEAD_DIM, (h + 1) * HEAD_DIM)

    def chunk_rows(c):
        return slice(c * RET_CHUNK, (c + 1) * RET_CHUNK)

    def norm_input(x, slot):
        ms = jnp.mean(x * x, axis=-1, keepdims=True)
        xn = x * lax.rsqrt(ms + NORM_EPS) * gin_ref[...]
        xn_scr[slot] = xn.astype(BF16)
        for nb in range(D_MODEL // LRU_BLOCK_DIM):
            xslab_scr[nb] = xn[:, nb * LRU_BLOCK_DIM:(nb + 1) * LRU_BLOCK_DIM]
        pieces = []
        for g in range(GROUPS_PER_HALF):
            for i in range(RUN):
                rows = pl.ds(g * GROUP_ROWS + i, V7X_SUBLANES, stride=RUN)
                pieces.append(jnp.concatenate(
                    [xslab_scr[nb, rows, :] for nb in range(D_MODEL // LRU_BLOCK_DIM)], axis=1))
        xnp_scr[slot] = jnp.concatenate(pieces, axis=0).astype(BF16)

    def in_piece(slot, p, cos_t, sin_t, tail_is_zero):
        group, part = divmod(p, PIECES_PER_GROUP)
        cols = slice(p * MXU_PIECE_COLS, (p + 1) * MXU_PIECE_COLS)
        lanes = slice(part * MXU_PIECE_COLS, (part + 1) * MXU_PIECE_COLS)
        lhs = xnp_scr[slot] if group in (XLRU_GROUP, GLRU_GROUP) else xn_scr[slot]
        val = jnp.dot(lhs, win_scr[:, cols], preferred_element_type=F32)
        if group == Q_GROUP:
            for j in range(HEADS_PER_PIECE):
                hl = head_lanes(j)
                gl = head_lanes(part * HEADS_PER_PIECE + j)
                q_scr[slot, :, gl] = _rotate_half_apply(val[:, hl], cos_t, sin_t).astype(BF16)
        elif group == K_GROUP:
            for j in range(HEADS_PER_PIECE):
                hl = head_lanes(j)
                gl = head_lanes(part * HEADS_PER_PIECE + j)
                kr = _rotate_half_apply(val[:, hl], cos_t, sin_t)
                for c in range(CHUNKS_PER_HALF):
                    kc = kr[chunk_rows(c)] * kscale_ref[:, gl]
                    kt_scr[gl, chunk_rows(c)] = kc.T.astype(BF16)
        elif group == V_GROUP:
            v_scr[slot, :, lanes] = val.astype(BF16)
        elif group == GRET_GROUP:
            gret_scr[slot, :, lanes] = val
        elif group == GLRU_GROUP:
            glru_scr[slot, :, lanes] = val
        else:
            taps = CONV_WIDTH - 1
            x4 = val.reshape(GROUPS_PER_HALF, RUN, V7X_SUBLANES, MXU_PIECE_COLS)
            prev_tail = tail_scr[:, :, lanes]
            if tail_is_zero is not None:
                prev_tail = jnp.where(tail_is_zero, 0.0, prev_tail)
            tail_scr[:, :, lanes] = x4[GROUPS_PER_HALF - 1, RUN - taps:RUN]
            late = x4[:, RUN - taps:RUN]
            late_prev = jnp.concatenate([prev_tail[None], late[:-1]], axis=0)
            flat = (GROUPS_PER_HALF * taps, V7X_SUBLANES, MXU_PIECE_COLS)
            sub = lax.broadcasted_iota(jnp.int32, flat, 1)
            wrapped = jnp.where(sub >= 1, pltpu.roll(late.reshape(flat), 1, 1),
                                pltpu.roll(late_prev.reshape(flat), 1, 1)).reshape(late.shape)
            xc4 = convb_ref[:, lanes] + x4 * convw_ref[taps:CONV_WIDTH, lanes]
            for s in range(1, CONV_WIDTH):
                src = jnp.concatenate([wrapped[:, taps - s:], x4[:, :RUN - s]], axis=1)
                xc4 = xc4 + src * convw_ref[taps - s:CONV_WIDTH - s, lanes]
            xc = xc4.reshape(hb, MXU_PIECE_COLS)
            xc_scr[slot, :, lanes] = xc
            xcb_scr[slot, :, lanes] = xc.astype(BF16)

    def block_diagonal(top_left, bottom_right):
        zeros = jnp.zeros_like(top_left)
        return jnp.concatenate([jnp.concatenate([top_left, zeros], axis=1),
                                jnp.concatenate([zeros, bottom_right], axis=1)], axis=0)

    def retention_scores(slot, hp, c):
        rs = chunk_rows(c)
        pair = slice(2 * hp * HEAD_DIM, (2 * hp + 2) * HEAD_DIM)
        keys = block_diagonal(kt_scr[head_lanes(2 * hp), rs], kt_scr[head_lanes(2 * hp + 1), rs])
        scores = jnp.dot(q_scr[slot, rs, pair], keys, preferred_element_type=F32)
        mask = jnp.concatenate([causal, causal], axis=1)
        sd_scr[slot, rs, pair] = jnp.where(mask, scores, 0.0).astype(BF16)

    def retention_summaries(slot, h):
        hl = head_lanes(h)
        values = block_diagonal(v_scr[slot, chunk_rows(0), hl], v_scr[slot, chunk_rows(1), hl])
        kv = jnp.dot(kt_scr[hl, :], values, preferred_element_type=F32)
        kv_scr[slot, 0, h] = kv[:, :HEAD_DIM]
        kv_scr[slot, 1, h] = kv[:, HEAD_DIM:]

    def projection_stage(slot, cos_t, sin_t, tail_is_zero):
        piece = lambda p: functools.partial(in_piece, slot, p, cos_t, sin_t, tail_is_zero)
        heads = lambda part: range(part * HEADS_PER_PIECE, (part + 1) * HEADS_PER_PIECE)
        prods = lambda part: (
            [functools.partial(retention_scores, slot, h // 2, c)
             for h in heads(part)[::2] for c in range(CHUNKS_PER_HALF)]
            + [functools.partial(retention_summaries, slot, h) for h in heads(part)])
        order = []
        for part in range(PIECES_PER_GROUP):
            order += [piece(g * PIECES_PER_GROUP + part) for g in (Q_GROUP, K_GROUP, V_GROUP)]
            order += prods(part)
        order += [piece(g * PIECES_PER_GROUP + part) for g in (GRET_GROUP, XLRU_GROUP, GLRU_GROUP)
                  for part in range(PIECES_PER_GROUP)]
        return order

    def retention_unit(slot, h, c):
        hl, rs = head_lanes(h), chunk_rows(c)
        state = state_scr[h]
        lhs = jnp.concatenate([sd_scr[slot, rs, hl], q_scr[slot, rs, hl]], axis=1)
        rhs = jnp.concatenate([v_scr[slot, rs, hl], state.astype(BF16)], axis=0)
        o = jnp.dot(lhs, rhs, preferred_element_type=F32) * xi_ref[:, hl]
        state_scr[h] = gdec_ref[:, hl] * (state + kv_scr[slot, c, h])
        mu = jnp.mean(o, axis=-1, keepdims=True)
        oc = o - mu
        var = jnp.mean(oc * oc, axis=-1, keepdims=True)
        gate = gret_scr[slot, rs, hl]
        y = oc * lax.rsqrt(var + NORM_EPS) * (gate * _sigmoid(gate))
        mixed_scr[slot, rs, hl] = y.astype(BF16)

    def lru_unit(slot, n, rh):
        sl = slice(n * LRU_BLOCK_DIM, (n + 1) * LRU_BLOCK_DIM)
        rows = slice(rh * LRU_ROWS, (rh + 1) * LRU_ROWS)
        xc = xc_scr[slot, rows, sl]
        pre = jnp.dot(xcb_scr[slot, rows, sl], wg_ref[n], preferred_element_type=F32)
        z = -lam_ref[:, sl]
        softplus = jnp.maximum(z, 0.0) + jnp.log1p(jnp.exp(-jnp.abs(z)))
        r = _sigmoid(pre[:, :LRU_BLOCK_DIM] + ba_ref[:, sl])
        i = _sigmoid(pre[:, LRU_BLOCK_DIM:] + bx_ref[:, sl])
        a = jnp.exp2(r * ((-LRU_C * LOG2_E) * softplus))
        v = 1.0 - a * a
        mult = jnp.where(v > 0.0, v * lax.rsqrt(v), 0.0)
        b = mult * (i * xc)
        hseq, hcarry_scr[:, sl] = _linear_scan_interleaved(a, b, hcarry_scr[:, sl])
        gate = glru_scr[slot, rows, sl]
        y = hseq * (gate * _sigmoid(gate))
        for g in range(LRU_ROWS // GROUP_ROWS):
            for i in range(RUN):
                src = slice(g * GROUP_ROWS + i * V7X_SUBLANES, g * GROUP_ROWS + (i + 1) * V7X_SUBLANES)
                dst = pl.ds(rows.start + g * GROUP_ROWS + i, V7X_SUBLANES, stride=RUN)
                yslab_scr[slot, n, dst, :] = y[src]

    def pack_lru_outputs(half):
        for n in range(LRU_BLOCKS):
            osl = slice(D_MODEL + n * LRU_BLOCK_DIM, D_MODEL + (n + 1) * LRU_BLOCK_DIM)
            mixed_scr[half, :, osl] = yslab_scr[half, n].astype(BF16)

    def mixer_stage(slot):
        units = []
        for n in range(LRU_BLOCKS):
            for c in range(CHUNKS_PER_HALF):
                units.append(functools.partial(lru_unit, slot, n, c))
                units.append(functools.partial(retention_unit, slot, n, c))
        return units

    def out_piece(half, p):
        cols = slice(p * MXU_PIECE_COLS, (p + 1) * MXU_PIECE_COLS)
        resid = x_ref[0, 0:hb, cols] if half == 0 else xstash_scr[:, cols]
        hres_scr[:, cols] = resid + jnp.dot(mixed_scr[half], wout_scr[:, cols],
                                            preferred_element_type=F32)

    def norm_output(half):
        hres = hres_scr[...]
        ms = jnp.mean(hres * hres, axis=-1, keepdims=True)
        y = hres * lax.rsqrt(ms + NORM_EPS) * gout_ref[...]
        if half == 0:
            out_stage[...] = y
        else:
            out_ref[0, hb:2 * hb, :] = y

    def output_stage(half):
        return [functools.partial(pack_lru_outputs, half)] + [
            functools.partial(out_piece, half, p) for p in range(PIECES_PER_GROUP)] + [
            functools.partial(norm_output, half)]

    @pl.when(step < n_steps)
    def _():
        @pl.when(step == 0)
        def _():
            stages = [buf.at[slot] for buf in (gret_scr, xc_scr, glru_scr) for slot in range(2)]
            _load_weights_as_bf16([(win_hbm, win_scr), (wout_hbm, wout_scr)], stages, weight_sems)
            norm_input(x_ref[0, 0:hb, :], 0)
            tail_scr[...] = jnp.zeros_like(tail_scr)
            for thunk in projection_stage(0, cos_ref[0:hb, :], sin_ref[0:hb, :], None):
                thunk()
            mixed_scr[1] = jnp.zeros(mixed_scr.shape[1:], BF16)
            yslab_scr[1] = jnp.zeros(yslab_scr.shape[1:], F32)
            xstash_scr[...] = jnp.zeros_like(xstash_scr)
            out_stage[...] = jnp.zeros_like(out_stage)

        @pl.when(step % steps_per_row == 0)
        def _():
            state_scr[...] = jnp.zeros_like(state_scr)
            hcarry_scr[...] = jnp.zeros_like(hcarry_scr)

        out_ref[0, 0:hb, :] = out_stage[...]
        norm_input(x_ref[0, hb:2 * hb, :], 1)
        matrix = output_stage(1) + projection_stage(1, cos_ref[hb:2 * hb, :], sin_ref[hb:2 * hb, :], None)
        for thunk in _interleave(mixer_stage(0), matrix):
            thunk()
        xstash_scr[...] = x_ref[0, hb:2 * hb, :]
        norm_input(xnext_ref[0], 0)
        next_starts_row = (step + 1) % steps_per_row == 0
        matrix = output_stage(0) + projection_stage(0, cosn_ref[...], sinn_ref[...], next_starts_row)
        for thunk in _interleave(mixer_stage(1), matrix):
            thunk()

    @pl.when(step == n_steps)
    def _():
        out_ref[0, 0:hb, :] = out_stage[...]
        for thunk in output_stage(1):
            thunk()


def _position_tables(seq_len):
    half = np.arange(0, HEAD_DIM, 2, dtype=np.float64)
    inv_freq = ROPE_BASE ** (-half / HEAD_DIM)
    ang = np.arange(seq_len, dtype=np.float64)[:, None] * inv_freq[None, :]
    cos, sin = np.cos(ang), np.sin(ang)
    cos_t = np.concatenate([cos, cos], axis=-1).astype(np.float32)
    sin_t = np.concatenate([-sin, sin], axis=-1).astype(np.float32)
    return cos_t, sin_t


def _decay_tables():
    c = RET_CHUNK
    scale = HEAD_DIM ** -0.5
    log_g = np.log1p(-np.exp2(-5.0 - np.arange(RET_HEADS, dtype=np.float64)))
    idx = np.arange(c, dtype=np.float64)
    xi = np.exp((idx + 1)[None, :] * log_g[:, None])
    kscale = scale * np.exp(-(idx + 1)[None, :] * log_g[:, None])
    gdec = np.exp(c * log_g)

    def rows_by_head_lanes(t):
        return np.repeat(t.T[:, :, None], HEAD_DIM, axis=2).reshape(c, RET_HEADS * HEAD_DIM)

    f32 = lambda t: t.astype(np.float32)
    return (f32(rows_by_head_lanes(kscale)), f32(rows_by_head_lanes(xi)),
            f32(np.repeat(gdec, HEAD_DIM)[None, :]))


def _resident(arr):
    nd = arr.ndim
    return pl.BlockSpec(arr.shape, lambda i: (0,) * nd, pipeline_mode=pl.Buffered(1))


_IN_HBM = pl.BlockSpec(memory_space=pl.ANY)


@jax.jit
def kernel(x, norm_in_g, w_in, conv_w, conv_b, gate_a_w, gate_a_b, gate_x_w, gate_x_b,
           lru_lambda, w_out, norm_out_g):
    batch, seq_len, d_model = x.shape
    assert d_model == D_MODEL and seq_len % STEP_BLOCK == 0
    assert w_in.shape == (D_MODEL, IN_WIDTH) and w_out.shape == (MIX_WIDTH, D_MODEL)
    steps_per_row = seq_len // STEP_BLOCK
    n_steps = batch * steps_per_row
    halves_per_row = 2 * steps_per_row

    cos_t, sin_t = _position_tables(seq_len)
    kscale_t, xi_t, gdec_t = _decay_tables()
    w_gates = jnp.concatenate([gate_a_w, gate_x_w], axis=-1).astype(BF16)
    row = lambda p: p.reshape(1, D_MODEL).astype(F32)

    def next_half(i):
        n = jnp.minimum(2 * (i + 1), 2 * n_steps - 2)
        return (n // halves_per_row, n % halves_per_row, 0)

    def block_of_step(i):
        j = jnp.minimum(i, n_steps - 1)
        return (j // steps_per_row, j % steps_per_row, 0)

    def block_of_previous_step(i):
        j = jnp.maximum(i - 1, 0)
        return (j // steps_per_row, j % steps_per_row, 0)

    step_rows = pl.BlockSpec((1, STEP_BLOCK, D_MODEL), block_of_step)
    step_table = pl.BlockSpec((STEP_BLOCK, HEAD_DIM), lambda i: block_of_step(i)[1:])
    next_table = pl.BlockSpec((HALF_BLOCK, HEAD_DIM), lambda i: next_half(i)[1:])
    small = (row(norm_in_g), conv_w.astype(F32), row(conv_b), w_gates, row(gate_a_b),
             row(gate_x_b), row(lru_lambda), row(norm_out_g), kscale_t, xi_t, gdec_t)
    (gin, convw, convb, wg, ba, bx, lam, gout, kscale_t, xi_t, gdec_t) = small
    operands_and_specs = (
        (x, step_rows), (x, pl.BlockSpec((1, HALF_BLOCK, D_MODEL), next_half)),
        (cos_t, step_table), (sin_t, step_table), (cos_t, next_table), (sin_t, next_table),
        (gin, _resident(gin)), (w_in.astype(F32), _IN_HBM), (convw, _resident(convw)),
        (convb, _resident(convb)), (wg, _resident(wg)), (ba, _resident(ba)), (bx, _resident(bx)),
        (lam, _resident(lam)), (w_out.astype(F32), _IN_HBM), (gout, _resident(gout)),
        (kscale_t, _resident(kscale_t)), (xi_t, _resident(xi_t)), (gdec_t, _resident(gdec_t)),
    )
    operands = [op for op, _ in operands_and_specs]
    in_specs = [spec for _, spec in operands_and_specs]

    half_f32 = pltpu.VMEM((2, HALF_BLOCK, D_MODEL), F32)
    half_bf16 = pltpu.VMEM((2, HALF_BLOCK, D_MODEL), BF16)
    return pl.pallas_call(
        functools.partial(_layer_body, steps_per_row=steps_per_row, n_steps=n_steps),
        grid=(n_steps + 1,),
        in_specs=in_specs,
        out_specs=pl.BlockSpec((1, STEP_BLOCK, D_MODEL), block_of_previous_step),
        out_shape=jax.ShapeDtypeStruct(x.shape, x.dtype),
        scratch_shapes=[
            pltpu.VMEM((D_MODEL, IN_WIDTH), BF16),
            pltpu.VMEM((MIX_WIDTH, D_MODEL), BF16),
            pltpu.SemaphoreType.DMA((WEIGHT_STAGES,)),
            half_bf16,
            pltpu.VMEM((D_MODEL, HALF_BLOCK), BF16),
            half_bf16,
            half_bf16,
            pltpu.VMEM((2, CHUNKS_PER_HALF, RET_HEADS, HEAD_DIM, HEAD_DIM), F32),
            half_f32,
            half_f32,
            half_bf16,
            half_f32,
            half_bf16,
            half_bf16,
            pltpu.VMEM((D_MODEL // LRU_BLOCK_DIM, HALF_BLOCK, LRU_BLOCK_DIM), F32),
            pltpu.VMEM((2, LRU_BLOCKS, HALF_BLOCK, LRU_BLOCK_DIM), F32),
            pltpu.VMEM((HALF_BLOCK, D_MODEL), F32),
            pltpu.VMEM((RET_HEADS, HEAD_DIM, HEAD_DIM), F32),
            pltpu.VMEM((1, D_MODEL), F32),
            pltpu.VMEM((CONV_WIDTH - 1, V7X_SUBLANES, D_MODEL), F32),
            pltpu.VMEM((2, HALF_BLOCK, MIX_WIDTH), BF16),
            pltpu.VMEM((HALF_BLOCK, D_MODEL), F32),
            pltpu.VMEM((HALF_BLOCK, D_MODEL), F32),
        ],
        compiler_params=pltpu.CompilerParams(
            dimension_semantics=("arbitrary",),
            vmem_limit_bytes=V7X_VMEM_LIMIT_BYTES,
        ),
        name="hybrid_layer",
    )(*operands)
```

```python
import functools

import jax
import jax.numpy as jnp
import numpy as np
from jax import lax
from jax.experimental import pallas as pl
from jax.experimental.pallas import tpu as pltpu

D_MODEL = 1024
RET_HEADS = 8
HEAD_DIM = 128
RET_CHUNK = 128
ROPE_BASE = 10000.0
LRU_BLOCKS = 8
LRU_BLOCK_DIM = 128
LRU_C = 8.0
CONV_WIDTH = 4
MIX_WIDTH = 2 * D_MODEL
IN_WIDTH = 6 * D_MODEL
NORM_EPS = 1e-6

V7X_SUBLANES = 8
HALF_BLOCK = 256
STEP_BLOCK = 2 * HALF_BLOCK
CHUNKS_PER_HALF = HALF_BLOCK // RET_CHUNK
assert CHUNKS_PER_HALF == 2 and RET_HEADS % 2 == 0
MXU_PIECE_COLS = 512
HEADS_PER_PIECE = MXU_PIECE_COLS // HEAD_DIM
PIECES_PER_GROUP = D_MODEL // MXU_PIECE_COLS
RUN = V7X_SUBLANES
GROUP_ROWS = RUN * V7X_SUBLANES
GROUPS_PER_HALF = HALF_BLOCK // GROUP_ROWS
LRU_ROWS = GROUP_ROWS
WEIGHT_BLOCK_ROWS = HALF_BLOCK
WEIGHT_STAGES = 6
V7X_VMEM_BYTES = 64 * 1024 * 1024
COMPILER_SCRATCH_BYTES = 8 * 1024 * 1024

Q_GROUP, K_GROUP, V_GROUP, GRET_GROUP, XLRU_GROUP, GLRU_GROUP = range(6)

F32 = jnp.float32
BF16 = jnp.bfloat16
LOG2_E = 1.4426950408889634


def _sigmoid(z):
    return 1.0 / (1.0 + jnp.exp2(z * (-LOG2_E)))


def _rotate_half_apply(xh, cos_t, sin_t):
    return xh * cos_t + pltpu.roll(xh, HEAD_DIM // 2, 1) * sin_t


def _linear_scan_interleaved(a, b, h0):
    lanes = a.shape[1]
    n_groups = a.shape[0] // GROUP_ROWS
    a4 = a.reshape(n_groups, RUN, V7X_SUBLANES, lanes)
    b4 = b.reshape(n_groups, RUN, V7X_SUBLANES, lanes)
    sub = lax.broadcasted_iota(jnp.int32, (V7X_SUBLANES, lanes), 0)
    carry = jnp.broadcast_to(h0, (V7X_SUBLANES, lanes))
    out = []
    for g in range(n_groups):
        h_loc, a_cum = [b4[g, 0]], [a4[g, 0]]
        for i in range(1, RUN):
            h_loc.append(a4[g, i] * h_loc[-1] + b4[g, i])
            a_cum.append(a4[g, i] * a_cum[-1])
        p, e = a_cum[-1], h_loc[-1]
        shift = 1
        while shift < V7X_SUBLANES:
            keep = sub >= shift
            p_prev = jnp.where(keep, pltpu.roll(p, shift, 0), 1.0)
            e_prev = jnp.where(keep, pltpu.roll(e, shift, 0), 0.0)
            e = p * e_prev + e
            p = p * p_prev
            shift *= 2
        first = sub >= 1
        entry = (jnp.where(first, pltpu.roll(e, 1, 0), 0.0)
                 + jnp.where(first, pltpu.roll(p, 1, 0), 1.0) * carry)
        hs = [h_loc[i] + a_cum[i] * entry for i in range(RUN)]
        out += hs
        last = hs[-1][V7X_SUBLANES - 1:V7X_SUBLANES, :]
        carry = jnp.broadcast_to(last, (V7X_SUBLANES, lanes))
    return jnp.concatenate(out, axis=0), last


def _load_weights_as_bf16(weights, stages, sems):
    blocks = [(w_hbm, w_scr, r, c)
              for w_hbm, w_scr in weights
              for r in range(w_hbm.shape[0] // WEIGHT_BLOCK_ROWS)
              for c in range(w_hbm.shape[1] // D_MODEL)]
    n_stage = len(stages)

    def block_copy(k):
        w_hbm, _, r, c = blocks[k]
        src = w_hbm.at[pl.ds(r * WEIGHT_BLOCK_ROWS, WEIGHT_BLOCK_ROWS), pl.ds(c * D_MODEL, D_MODEL)]
        return pltpu.make_async_copy(src, stages[k % n_stage], sems.at[k % n_stage])

    for k in range(min(n_stage, len(blocks))):
        block_copy(k).start()
    for k, (_, w_scr, r, c) in enumerate(blocks):
        block_copy(k).wait()
        w_scr[r * WEIGHT_BLOCK_ROWS:(r + 1) * WEIGHT_BLOCK_ROWS, c * D_MODEL:(c + 1) * D_MODEL] = (
            stages[k % n_stage][...].astype(BF16))
        if k + n_stage < len(blocks):
            block_copy(k + n_stage).start()


def _interleave(vector_units, matrix_units):
    order, i, j = [], 0, 0
    nv, nm = max(len(vector_units), 1), max(len(matrix_units), 1)
    while i < len(vector_units) or j < len(matrix_units):
        if j < len(matrix_units) and (i >= len(vector_units) or j * nv <= i * nm):
            order.append(matrix_units[j])
            j += 1
        else:
            order.append(vector_units[i])
            i += 1
    return order


def _layer_body(x_ref, xnext_ref, cos_ref, sin_ref, cosn_ref, sinn_ref, gin_ref, win_hbm,
                convw_ref, convb_ref, wg_ref, ba_ref, bx_ref, lam_ref, wout_hbm, gout_ref,
                kscale_ref, xi_ref, gdec_ref, out_ref,
                win_scr, wout_scr, weight_sems,
                q_scr, kt_scr, v_scr, sd_scr, kv_scr, gret_scr, xc_scr, xcb_scr, glru_scr,
                xn_scr, xnp_scr, xslab_scr, yslab_scr, hres_scr, state_scr, hcarry_scr, tail_scr,
                mixed_scr, xstash_scr, out_stage, *, steps_per_row, n_steps):
    hb = HALF_BLOCK
    step = pl.program_id(0)
    causal = (lax.broadcasted_iota(jnp.int32, (RET_CHUNK, RET_CHUNK), 0)
              >= lax.broadcasted_iota(jnp.int32, (RET_CHUNK, RET_CHUNK), 1))

    def head_lanes(h):
        return slice(h * HEAD_DIM, (h + 1) * HEAD_DIM)

    def chunk_rows(c):
        return slice(c * RET_CHUNK, (c + 1) * RET_CHUNK)

    def norm_input(x, slot):
        ms = jnp.mean(x * x, axis=-1, keepdims=True)
        inv_rms = lax.rsqrt(ms + NORM_EPS)
        for nb in range(D_MODEL // LRU_BLOCK_DIM):
            sl = slice(nb * LRU_BLOCK_DIM, (nb + 1) * LRU_BLOCK_DIM)
            xn = x[:, sl] * inv_rms * gin_ref[:, sl]
            xn_scr[slot, :, sl] = xn.astype(BF16)
            xslab_scr[nb] = xn
        pieces = []
        for g in range(GROUPS_PER_HALF):
            for i in range(RUN):
                rows = pl.ds(g * GROUP_ROWS + i, V7X_SUBLANES, stride=RUN)
                pieces.append(jnp.concatenate(
                    [xslab_scr[nb, rows, :] for nb in range(D_MODEL // LRU_BLOCK_DIM)], axis=1))
        xnp_scr[slot] = jnp.concatenate(pieces, axis=0).astype(BF16)

    def in_piece(slot, p, cos_t, sin_t, tail_is_zero):
        group, part = divmod(p, PIECES_PER_GROUP)
        cols = slice(p * MXU_PIECE_COLS, (p + 1) * MXU_PIECE_COLS)
        lanes = slice(part * MXU_PIECE_COLS, (part + 1) * MXU_PIECE_COLS)
        lhs = xnp_scr[slot] if group in (XLRU_GROUP, GLRU_GROUP) else xn_scr[slot]
        val = jnp.dot(lhs, win_scr[:, cols], preferred_element_type=F32)
        if group == Q_GROUP:
            for j in range(HEADS_PER_PIECE):
                hl = head_lanes(j)
                gl = head_lanes(part * HEADS_PER_PIECE + j)
                q_scr[slot, :, gl] = _rotate_half_apply(val[:, hl], cos_t, sin_t).astype(BF16)
        elif group == K_GROUP:
            for j in range(HEADS_PER_PIECE):
                hl = head_lanes(j)
                gl = head_lanes(part * HEADS_PER_PIECE + j)
                kr = _rotate_half_apply(val[:, hl], cos_t, sin_t)
                for c in range(CHUNKS_PER_HALF):
                    kc = kr[chunk_rows(c)] * kscale_ref[:, gl]
                    kt_scr[gl, chunk_rows(c)] = kc.T.astype(BF16)
        elif group == V_GROUP:
            v_scr[slot, :, lanes] = val.astype(BF16)
        elif group == GRET_GROUP:
            gret_scr[slot, :, lanes] = val
        elif group == GLRU_GROUP:
            glru_scr[slot, :, lanes] = val
        else:
            taps = CONV_WIDTH - 1
            x4 = val.reshape(GROUPS_PER_HALF, RUN, V7X_SUBLANES, MXU_PIECE_COLS)
            prev_tail = tail_scr[:, :, lanes]
            if tail_is_zero is not None:
                prev_tail = jnp.where(tail_is_zero, 0.0, prev_tail)
            tail_scr[:, :, lanes] = x4[GROUPS_PER_HALF - 1, RUN - taps:RUN]
            late = x4[:, RUN - taps:RUN]
            late_prev = jnp.concatenate([prev_tail[None], late[:-1]], axis=0)
            flat = (GROUPS_PER_HALF * taps, V7X_SUBLANES, MXU_PIECE_COLS)
            sub = lax.broadcasted_iota(jnp.int32, flat, 1)
            wrapped = jnp.where(sub >= 1, pltpu.roll(late.reshape(flat), 1, 1),
                                pltpu.roll(late_prev.reshape(flat), 1, 1)).reshape(late.shape)
            xc4 = convb_ref[:, lanes] + x4 * convw_ref[taps:CONV_WIDTH, lanes]
            for s in range(1, CONV_WIDTH):
                src = jnp.concatenate([wrapped[:, taps - s:], x4[:, :RUN - s]], axis=1)
                xc4 = xc4 + src * convw_ref[taps - s:CONV_WIDTH - s, lanes]
            xc = xc4.reshape(hb, MXU_PIECE_COLS)
            xc_scr[slot, :, lanes] = xc
            xcb_scr[slot, :, lanes] = xc.astype(BF16)

    def block_diagonal(top_left, bottom_right):
        zeros = jnp.zeros_like(top_left)
        return jnp.concatenate([jnp.concatenate([top_left, zeros], axis=1),
                                jnp.concatenate([zeros, bottom_right], axis=1)], axis=0)

    def retention_scores(slot, hp, c):
        rs = chunk_rows(c)
        pair = slice(2 * hp * HEAD_DIM, (2 * hp + 2) * HEAD_DIM)
        keys = block_diagonal(kt_scr[head_lanes(2 * hp), rs], kt_scr[head_lanes(2 * hp + 1), rs])
        scores = jnp.dot(q_scr[slot, rs, pair], keys, preferred_element_type=F32)
        mask = jnp.concatenate([causal, causal], axis=1)
        sd_scr[slot, rs, pair] = jnp.where(mask, scores, 0.0).astype(BF16)

    def retention_summaries(slot, h):
        hl = head_lanes(h)
        values = block_diagonal(v_scr[slot, chunk_rows(0), hl], v_scr[slot, chunk_rows(1), hl])
        kv = jnp.dot(kt_scr[hl, :], values, preferred_element_type=F32)
        kv_scr[slot, 0, h] = kv[:, :HEAD_DIM]
        kv_scr[slot, 1, h] = kv[:, HEAD_DIM:]

    def projection_stage(slot, cos_t, sin_t, tail_is_zero):
        piece = lambda p: functools.partial(in_piece, slot, p, cos_t, sin_t, tail_is_zero)
        heads = lambda part: range(part * HEADS_PER_PIECE, (part + 1) * HEADS_PER_PIECE)
        prods = lambda part: (
            [functools.partial(retention_scores, slot, h // 2, c)
             for h in heads(part)[::2] for c in range(CHUNKS_PER_HALF)]
            + [functools.partial(retention_summaries, slot, h) for h in heads(part)])
        order = []
        for part in range(PIECES_PER_GROUP):
            order += [piece(g * PIECES_PER_GROUP + part) for g in (Q_GROUP, K_GROUP, V_GROUP)]
            order += prods(part)
        order += [piece(g * PIECES_PER_GROUP + part) for g in (GRET_GROUP, XLRU_GROUP, GLRU_GROUP)
                  for part in range(PIECES_PER_GROUP)]
        return order

    def retention_unit(slot, h, c):
        hl, rs = head_lanes(h), chunk_rows(c)
        state = state_scr[h]
        lhs = jnp.concatenate([sd_scr[slot, rs, hl], q_scr[slot, rs, hl]], axis=1)
        rhs = jnp.concatenate([v_scr[slot, rs, hl], state.astype(BF16)], axis=0)
        o = jnp.dot(lhs, rhs, preferred_element_type=F32) * xi_ref[:, hl]
        state_scr[h] = gdec_ref[:, hl] * (state + kv_scr[slot, c, h])
        mu = jnp.mean(o, axis=-1, keepdims=True)
        oc = o - mu
        var = jnp.mean(oc * oc, axis=-1, keepdims=True)
        gate = gret_scr[slot, rs, hl]
        y = oc * lax.rsqrt(var + NORM_EPS) * (gate * _sigmoid(gate))
        mixed_scr[slot, rs, hl] = y.astype(BF16)

    def lru_unit(slot, n, rh):
        sl = slice(n * LRU_BLOCK_DIM, (n + 1) * LRU_BLOCK_DIM)
        rows = slice(rh * LRU_ROWS, (rh + 1) * LRU_ROWS)
        xc = xc_scr[slot, rows, sl]
        pre = jnp.dot(xcb_scr[slot, rows, sl], wg_ref[n], preferred_element_type=F32)
        z = -lam_ref[:, sl]
        softplus = jnp.maximum(z, 0.0) + jnp.log1p(jnp.exp(-jnp.abs(z)))
        r = _sigmoid(pre[:, :LRU_BLOCK_DIM] + ba_ref[:, sl])
        i = _sigmoid(pre[:, LRU_BLOCK_DIM:] + bx_ref[:, sl])
        a = jnp.exp2(r * ((-LRU_C * LOG2_E) * softplus))
        v = 1.0 - a * a
        mult = jnp.where(v > 0.0, v * lax.rsqrt(v), 0.0)
        b = mult * (i * xc)
        hseq, hcarry_scr[:, sl] = _linear_scan_interleaved(a, b, hcarry_scr[:, sl])
        gate = glru_scr[slot, rows, sl]
        y = hseq * (gate * _sigmoid(gate))
        for g in range(LRU_ROWS // GROUP_ROWS):
            for i in range(RUN):
                src = slice(g * GROUP_ROWS + i * V7X_SUBLANES, g * GROUP_ROWS + (i + 1) * V7X_SUBLANES)
                dst = pl.ds(rows.start + g * GROUP_ROWS + i, V7X_SUBLANES, stride=RUN)
                yslab_scr[slot, n, dst, :] = y[src]

    def pack_lru_outputs(half):
        for n in range(LRU_BLOCKS):
            osl = slice(D_MODEL + n * LRU_BLOCK_DIM, D_MODEL + (n + 1) * LRU_BLOCK_DIM)
            mixed_scr[half, :, osl] = yslab_scr[half, n].astype(BF16)

    def mixer_stage(slot):
        lru = [functools.partial(lru_unit, slot, n, rh)
               for n in range(LRU_BLOCKS) for rh in range(HALF_BLOCK // LRU_ROWS)]
        ret = [functools.partial(retention_unit, slot, h, c)
               for h in range(RET_HEADS) for c in range(CHUNKS_PER_HALF)]
        return _interleave(lru, ret)

    def out_piece(half, p):
        cols = slice(p * MXU_PIECE_COLS, (p + 1) * MXU_PIECE_COLS)
        resid = x_ref[0, 0:hb, cols] if half == 0 else xstash_scr[:, cols]
        hres_scr[:, cols] = resid + jnp.dot(mixed_scr[half], wout_scr[:, cols],
                                            preferred_element_type=F32)

    def norm_output(half):
        hres = hres_scr[...]
        ms = jnp.mean(hres * hres, axis=-1, keepdims=True)
        y = hres * lax.rsqrt(ms + NORM_EPS) * gout_ref[...]
        if half == 0:
            out_stage[...] = y
        else:
            out_ref[0, hb:2 * hb, :] = y

    def output_stage(half):
        return [functools.partial(pack_lru_outputs, half)] + [
            functools.partial(out_piece, half, p) for p in range(PIECES_PER_GROUP)] + [
            functools.partial(norm_output, half)]

    @pl.when(step < n_steps)
    def _():
        @pl.when(step == 0)
        def _():
            stages = [buf.at[slot] for buf in (gret_scr, xc_scr, glru_scr) for slot in range(2)]
            _load_weights_as_bf16([(win_hbm, win_scr), (wout_hbm, wout_scr)], stages, weight_sems)
            norm_input(x_ref[0, 0:hb, :], 0)
            tail_scr[...] = jnp.zeros_like(tail_scr)
            for thunk in projection_stage(0, cos_ref[0:hb, :], sin_ref[0:hb, :], None):
                thunk()
            mixed_scr[1] = jnp.zeros(mixed_scr.shape[1:], BF16)
            yslab_scr[1] = jnp.zeros(yslab_scr.shape[1:], F32)
            xstash_scr[...] = jnp.zeros_like(xstash_scr)
            out_stage[...] = jnp.zeros_like(out_stage)

        @pl.when(step % steps_per_row == 0)
        def _():
            state_scr[...] = jnp.zeros_like(state_scr)
            hcarry_scr[...] = jnp.zeros_like(hcarry_scr)

        out_ref[0, 0:hb, :] = out_stage[...]
        norm_input(x_ref[0, hb:2 * hb, :], 1)
        matrix = output_stage(1) + projection_stage(1, cos_ref[hb:2 * hb, :], sin_ref[hb:2 * hb, :], None)
        for thunk in _interleave(mixer_stage(0), matrix):
            thunk()
        xstash_scr[...] = x_ref[0, hb:2 * hb, :]
        norm_input(xnext_ref[0], 0)
        next_starts_row = (step + 1) % steps_per_row == 0
        matrix = output_stage(0) + projection_stage(0, cosn_ref[...], sinn_ref[...], next_starts_row)
        for thunk in _interleave(mixer_stage(1), matrix):
            thunk()

    @pl.when(step == n_steps)
    def _():
        out_ref[0, 0:hb, :] = out_stage[...]
        for thunk in output_stage(1):
            thunk()


def _position_tables(seq_len):
    half = np.arange(0, HEAD_DIM, 2, dtype=np.float64)
    inv_freq = ROPE_BASE ** (-half / HEAD_DIM)
    ang = np.arange(seq_len, dtype=np.float64)[:, None] * inv_freq[None, :]
    cos, sin = np.cos(ang), np.sin(ang)
    cos_t = np.concatenate([cos, cos], axis=-1).astype(np.float32)
    sin_t = np.concatenate([-sin, sin], axis=-1).astype(np.float32)
    return cos_t, sin_t


def _decay_tables():
    c = RET_CHUNK
    scale = HEAD_DIM ** -0.5
    log_g = np.log1p(-np.exp2(-5.0 - np.arange(RET_HEADS, dtype=np.float64)))
    idx = np.arange(c, dtype=np.float64)
    xi = np.exp((idx + 1)[None, :] * log_g[:, None])
    kscale = scale * np.exp(-(idx + 1)[None, :] * log_g[:, None])
    gdec = np.exp(c * log_g)

    def rows_by_head_lanes(t):
        return np.repeat(t.T[:, :, None], HEAD_DIM, axis=2).reshape(c, RET_HEADS * HEAD_DIM)

    f32 = lambda t: t.astype(np.float32)
    return (f32(rows_by_head_lanes(kscale)), f32(rows_by_head_lanes(xi)),
            f32(np.repeat(gdec, HEAD_DIM)[None, :]))


def _resident(arr):
    nd = arr.ndim
    return pl.BlockSpec(arr.shape, lambda i: (0,) * nd, pipeline_mode=pl.Buffered(1))


_IN_HBM = pl.BlockSpec(memory_space=pl.ANY)


@jax.jit
def kernel(x, norm_in_g, w_in, conv_w, conv_b, gate_a_w, gate_a_b, gate_x_w, gate_x_b,
           lru_lambda, w_out, norm_out_g):
    batch, seq_len, d_model = x.shape
    assert d_model == D_MODEL and seq_len % STEP_BLOCK == 0
    assert w_in.shape == (D_MODEL, IN_WIDTH) and w_out.shape == (MIX_WIDTH, D_MODEL)
    steps_per_row = seq_len // STEP_BLOCK
    n_steps = batch * steps_per_row
    halves_per_row = 2 * steps_per_row

    cos_t, sin_t = _position_tables(seq_len)
    kscale_t, xi_t, gdec_t = _decay_tables()
    w_gates = jnp.concatenate([gate_a_w, gate_x_w], axis=-1).astype(BF16)
    row = lambda p: p.reshape(1, D_MODEL).astype(F32)

    def next_half(i):
        n = jnp.minimum(2 * (i + 1), 2 * n_steps - 2)
        return (n // halves_per_row, n % halves_per_row, 0)

    def block_of_step(i):
        j = jnp.minimum(i, n_steps - 1)
        return (j // steps_per_row, j % steps_per_row, 0)

    def block_of_previous_step(i):
        j = jnp.maximum(i - 1, 0)
        return (j // steps_per_row, j % steps_per_row, 0)

    step_rows = pl.BlockSpec((1, STEP_BLOCK, D_MODEL), block_of_step)
    step_table = pl.BlockSpec((STEP_BLOCK, HEAD_DIM), lambda i: block_of_step(i)[1:])
    next_table = pl.BlockSpec((HALF_BLOCK, HEAD_DIM), lambda i: next_half(i)[1:])
    small = (row(norm_in_g), conv_w.astype(F32), row(conv_b), w_gates, row(gate_a_b),
             row(gate_x_b), row(lru_lambda), row(norm_out_g), kscale_t, xi_t, gdec_t)
    (gin, convw, convb, wg, ba, bx, lam, gout, kscale_t, xi_t, gdec_t) = small
    operands_and_specs = (
        (x, step_rows), (x, pl.BlockSpec((1, HALF_BLOCK, D_MODEL), next_half)),
        (cos_t, step_table), (sin_t, step_table), (cos_t, next_table), (sin_t, next_table),
        (gin, _resident(gin)), (w_in.astype(F32), _IN_HBM), (convw, _resident(convw)),
        (convb, _resident(convb)), (wg, _resident(wg)), (ba, _resident(ba)), (bx, _resident(bx)),
        (lam, _resident(lam)), (w_out.astype(F32), _IN_HBM), (gout, _resident(gout)),
        (kscale_t, _resident(kscale_t)), (xi_t, _resident(xi_t)), (gdec_t, _resident(gdec_t)),
    )
    operands = [op for op, _ in operands_and_specs]
    in_specs = [spec for _, spec in operands_and_specs]

    half_f32 = pltpu.VMEM((2, HALF_BLOCK, D_MODEL), F32)
    half_bf16 = pltpu.VMEM((2, HALF_BLOCK, D_MODEL), BF16)
    scratch_shapes = [
        pltpu.VMEM((D_MODEL, IN_WIDTH), BF16),
        pltpu.VMEM((MIX_WIDTH, D_MODEL), BF16),
        pltpu.SemaphoreType.DMA((WEIGHT_STAGES,)),
        half_bf16,
        pltpu.VMEM((D_MODEL, HALF_BLOCK), BF16),
        half_bf16,
        half_bf16,
        pltpu.VMEM((2, CHUNKS_PER_HALF, RET_HEADS, HEAD_DIM, HEAD_DIM), F32),
        half_f32,
        half_f32,
        half_bf16,
        half_f32,
        half_bf16,
        half_bf16,
        pltpu.VMEM((D_MODEL // LRU_BLOCK_DIM, HALF_BLOCK, LRU_BLOCK_DIM), F32),
        pltpu.VMEM((2, LRU_BLOCKS, HALF_BLOCK, LRU_BLOCK_DIM), F32),
        pltpu.VMEM((HALF_BLOCK, D_MODEL), F32),
        pltpu.VMEM((RET_HEADS, HEAD_DIM, HEAD_DIM), F32),
        pltpu.VMEM((1, D_MODEL), F32),
        pltpu.VMEM((CONV_WIDTH - 1, V7X_SUBLANES, D_MODEL), F32),
        pltpu.VMEM((2, HALF_BLOCK, MIX_WIDTH), BF16),
        pltpu.VMEM((HALF_BLOCK, D_MODEL), F32),
        pltpu.VMEM((HALF_BLOCK, D_MODEL), F32),
    ]

    nbytes = lambda shape, dtype: int(np.prod(shape)) * jnp.dtype(dtype).itemsize
    f32_rows = lambda rows, width: nbytes((rows, width), F32)
    windows = 2 * (f32_rows(2 * STEP_BLOCK + HALF_BLOCK, D_MODEL)
                   + 2 * f32_rows(STEP_BLOCK + HALF_BLOCK, HEAD_DIM))
    vmem_bytes = (sum(nbytes(s.shape, s.dtype) for s in scratch_shapes if s.memory_space == pltpu.VMEM)
                  + windows + sum(nbytes(a.shape, a.dtype) for a in small) + COMPILER_SCRATCH_BYTES)
    assert vmem_bytes <= V7X_VMEM_BYTES, vmem_bytes

    return pl.pallas_call(
        functools.partial(_layer_body, steps_per_row=steps_per_row, n_steps=n_steps),
        grid=(n_steps + 1,),
        in_specs=in_specs,
        out_specs=pl.BlockSpec((1, STEP_BLOCK, D_MODEL), block_of_previous_step),
        out_shape=jax.ShapeDtypeStruct(x.shape, x.dtype),
        scratch_shapes=scratch_shapes,
        compiler_params=pltpu.CompilerParams(
            dimension_semantics=("arbitrary",),
            vmem_limit_bytes=vmem_bytes,
        ),
        name="hybrid_layer",
    )(*operands)
```

```python
import functools

import jax
import jax.numpy as jnp
import numpy as np
from jax import lax
from jax.experimental import pallas as pl
from jax.experimental.pallas import tpu as pltpu

D_MODEL = 1024
RET_HEADS = 8
HEAD_DIM = 128
RET_CHUNK = 128
ROPE_BASE = 10000.0
LRU_BLOCKS = 8
LRU_BLOCK_DIM = 128
LRU_C = 8.0
CONV_WIDTH = 4
MIX_WIDTH = 2 * D_MODEL
IN_WIDTH = 6 * D_MODEL
NORM_EPS = 1e-6

V7X_SUBLANES = 8
HALF_BLOCK = 256
STEP_BLOCK = 2 * HALF_BLOCK
CHUNKS_PER_HALF = HALF_BLOCK // RET_CHUNK
assert CHUNKS_PER_HALF == 2 and RET_HEADS % 2 == 0
MXU_PIECE_COLS = 512
HEADS_PER_PIECE = MXU_PIECE_COLS // HEAD_DIM
PIECES_PER_GROUP = D_MODEL // MXU_PIECE_COLS
RUN = V7X_SUBLANES
GROUP_ROWS = RUN * V7X_SUBLANES
GROUPS_PER_HALF = HALF_BLOCK // GROUP_ROWS
LRU_ROWS = GROUP_ROWS
WEIGHT_BLOCK_ROWS = HALF_BLOCK
WEIGHT_STAGES = 6
V7X_VMEM_BYTES = 64 * 1024 * 1024
COMPILER_SCRATCH_BYTES = 8 * 1024 * 1024

Q_GROUP, K_GROUP, V_GROUP, GRET_GROUP, XLRU_GROUP, GLRU_GROUP = range(6)

F32 = jnp.float32
BF16 = jnp.bfloat16
LOG2_E = 1.4426950408889634


def _sigmoid(z):
    return 1.0 / (1.0 + jnp.exp2(z * (-LOG2_E)))


def _rotate_half_apply(xh, cos_t, sin_t):
    return xh * cos_t + pltpu.roll(xh, HEAD_DIM // 2, 1) * sin_t


def _linear_scan_interleaved(a, b, h0):
    lanes = a.shape[1]
    n_groups = a.shape[0] // GROUP_ROWS
    a4 = a.reshape(n_groups, RUN, V7X_SUBLANES, lanes)
    b4 = b.reshape(n_groups, RUN, V7X_SUBLANES, lanes)
    sub = lax.broadcasted_iota(jnp.int32, (V7X_SUBLANES, lanes), 0)
    carry = jnp.broadcast_to(h0, (V7X_SUBLANES, lanes))
    out = []
    for g in range(n_groups):
        h_loc, a_cum = [b4[g, 0]], [a4[g, 0]]
        for i in range(1, RUN):
            h_loc.append(a4[g, i] * h_loc[-1] + b4[g, i])
            a_cum.append(a4[g, i] * a_cum[-1])
        p, e = a_cum[-1], h_loc[-1]
        shift = 1
        while shift < V7X_SUBLANES:
            keep = sub >= shift
            p_prev = jnp.where(keep, pltpu.roll(p, shift, 0), 1.0)
            e_prev = jnp.where(keep, pltpu.roll(e, shift, 0), 0.0)
            e = p * e_prev + e
            p = p * p_prev
            shift *= 2
        first = sub >= 1
        entry = (jnp.where(first, pltpu.roll(e, 1, 0), 0.0)
                 + jnp.where(first, pltpu.roll(p, 1, 0), 1.0) * carry)
        hs = [h_loc[i] + a_cum[i] * entry for i in range(RUN)]
        out += hs
        last = hs[-1][V7X_SUBLANES - 1:V7X_SUBLANES, :]
        carry = jnp.broadcast_to(last, (V7X_SUBLANES, lanes))
    return jnp.concatenate(out, axis=0), last


def _load_weights_as_bf16(weights, stages, sems):
    blocks = [(w_hbm, w_scr, r, c)
              for w_hbm, w_scr in weights
              for r in range(w_hbm.shape[0] // WEIGHT_BLOCK_ROWS)
              for c in range(w_hbm.shape[1] // D_MODEL)]
    n_stage = len(stages)

    def block_copy(k):
        w_hbm, _, r, c = blocks[k]
        src = w_hbm.at[pl.ds(r * WEIGHT_BLOCK_ROWS, WEIGHT_BLOCK_ROWS), pl.ds(c * D_MODEL, D_MODEL)]
        return pltpu.make_async_copy(src, stages[k % n_stage], sems.at[k % n_stage])

    for k in range(min(n_stage, len(blocks))):
        block_copy(k).start()
    for k, (_, w_scr, r, c) in enumerate(blocks):
        block_copy(k).wait()
        w_scr[r * WEIGHT_BLOCK_ROWS:(r + 1) * WEIGHT_BLOCK_ROWS, c * D_MODEL:(c + 1) * D_MODEL] = (
            stages[k % n_stage][...].astype(BF16))
        if k + n_stage < len(blocks):
            block_copy(k + n_stage).start()


def _interleave(vector_units, matrix_units):
    order, i, j = [], 0, 0
    nv, nm = max(len(vector_units), 1), max(len(matrix_units), 1)
    while i < len(vector_units) or j < len(matrix_units):
        if j < len(matrix_units) and (i >= len(vector_units) or j * nv <= i * nm):
            order.append(matrix_units[j])
            j += 1
        else:
            order.append(vector_units[i])
            i += 1
    return order


def _layer_body(x_ref, xnext_ref, cos_ref, sin_ref, cosn_ref, sinn_ref, cost_ref, sint_ref,
                cosnt_ref, sinnt_ref, gin_ref, win_hbm,
                convw_ref, convb_ref, wg_ref, ba_ref, bx_ref, lam_ref, wout_hbm, gout_ref,
                kscale_ref, xi_ref, gdec_ref, out_ref,
                win_scr, wout_scr, weight_sems,
                q_scr, kt_scr, v_scr, sd_scr, kv_scr, gret_scr, xc_scr, xcb_scr, glru_scr,
                xn_scr, xnp_scr, xslab_scr, yslab_scr, hres_scr, state_scr, hcarry_scr, tail_scr,
                mixed_scr, xstash_scr, out_stage, *, steps_per_row, n_steps):
    hb = HALF_BLOCK
    step = pl.program_id(0)
    causal = (lax.broadcasted_iota(jnp.int32, (RET_CHUNK, RET_CHUNK), 0)
              >= lax.broadcasted_iota(jnp.int32, (RET_CHUNK, RET_CHUNK), 1))

    def head_lanes(h):
        return slice(h * HEAD_DIM, (h + 1) * HEAD_DIM)

    def chunk_rows(c):
        return slice(c * RET_CHUNK, (c + 1) * RET_CHUNK)

    def norm_input(x, slot):
        ms = jnp.mean(x * x, axis=-1, keepdims=True)
        inv_rms = lax.rsqrt(ms + NORM_EPS)
        for nb in range(D_MODEL // LRU_BLOCK_DIM):
            sl = slice(nb * LRU_BLOCK_DIM, (nb + 1) * LRU_BLOCK_DIM)
            xn = x[:, sl] * inv_rms * gin_ref[:, sl]
            xn_scr[slot, :, sl] = xn.astype(BF16)
            xslab_scr[nb] = xn
        pieces = []
        for g in range(GROUPS_PER_HALF):
            for i in range(RUN):
                rows = pl.ds(g * GROUP_ROWS + i, V7X_SUBLANES, stride=RUN)
                pieces.append(jnp.concatenate(
                    [xslab_scr[nb, rows, :] for nb in range(D_MODEL // LRU_BLOCK_DIM)], axis=1))
        xnp_scr[slot] = jnp.concatenate(pieces, axis=0).astype(BF16)

    def in_piece(slot, p, rot, tail_is_zero):
        group, part = divmod(p, PIECES_PER_GROUP)
        cols = slice(p * MXU_PIECE_COLS, (p + 1) * MXU_PIECE_COLS)
        lanes = slice(part * MXU_PIECE_COLS, (part + 1) * MXU_PIECE_COLS)
        lhs = xnp_scr[slot] if group in (XLRU_GROUP, GLRU_GROUP) else xn_scr[slot]
        val = jnp.dot(lhs, win_scr[:, cols], preferred_element_type=F32)
        cos_t, sin_t, cos_ft, sin_ft = rot
        if group == Q_GROUP:
            for j in range(HEADS_PER_PIECE):
                hl = head_lanes(j)
                gl = head_lanes(part * HEADS_PER_PIECE + j)
                q_scr[slot, :, gl] = _rotate_half_apply(val[:, hl], cos_t, sin_t).astype(BF16)
        elif group == K_GROUP:
            half_dim = HEAD_DIM // 2
            for j in range(HEADS_PER_PIECE):
                hl = head_lanes(j)
                h = part * HEADS_PER_PIECE + j
                for c in range(CHUNKS_PER_HALF):
                    rs = chunk_rows(c)
                    kt = val[rs, hl].T
                    swapped = jnp.concatenate([kt[half_dim:], kt[:half_dim]], axis=0)
                    kr = kt * cos_ft[:, rs] + swapped * sin_ft[:, rs]
                    kt_scr[head_lanes(h), rs] = (kr * kscale_ref[h:h + 1, :]).astype(BF16)
        elif group == V_GROUP:
            v_scr[slot, :, lanes] = val.astype(BF16)
        elif group == GRET_GROUP:
            gret_scr[slot, :, lanes] = val
        elif group == GLRU_GROUP:
            glru_scr[slot, :, lanes] = val
        else:
            taps = CONV_WIDTH - 1
            x4 = val.reshape(GROUPS_PER_HALF, RUN, V7X_SUBLANES, MXU_PIECE_COLS)
            prev_tail = tail_scr[:, :, lanes]
            if tail_is_zero is not None:
                prev_tail = jnp.where(tail_is_zero, 0.0, prev_tail)
            tail_scr[:, :, lanes] = x4[GROUPS_PER_HALF - 1, RUN - taps:RUN]
            late = x4[:, RUN - taps:RUN]
            late_prev = jnp.concatenate([prev_tail[None], late[:-1]], axis=0)
            flat = (GROUPS_PER_HALF * taps, V7X_SUBLANES, MXU_PIECE_COLS)
            sub = lax.broadcasted_iota(jnp.int32, flat, 1)
            wrapped = jnp.where(sub >= 1, pltpu.roll(late.reshape(flat), 1, 1),
                                pltpu.roll(late_prev.reshape(flat), 1, 1)).reshape(late.shape)
            xc4 = convb_ref[:, lanes] + x4 * convw_ref[taps:CONV_WIDTH, lanes]
            for s in range(1, CONV_WIDTH):
                src = jnp.concatenate([wrapped[:, taps - s:], x4[:, :RUN - s]], axis=1)
                xc4 = xc4 + src * convw_ref[taps - s:CONV_WIDTH - s, lanes]
            xc = xc4.reshape(hb, MXU_PIECE_COLS)
            xc_scr[slot, :, lanes] = xc
            xcb_scr[slot, :, lanes] = xc.astype(BF16)

    def block_diagonal(top_left, bottom_right):
        zeros = jnp.zeros_like(top_left)
        return jnp.concatenate([jnp.concatenate([top_left, zeros], axis=1),
                                jnp.concatenate([zeros, bottom_right], axis=1)], axis=0)

    def retention_scores(slot, hp, c):
        rs = chunk_rows(c)
        pair = slice(2 * hp * HEAD_DIM, (2 * hp + 2) * HEAD_DIM)
        keys = block_diagonal(kt_scr[head_lanes(2 * hp), rs], kt_scr[head_lanes(2 * hp + 1), rs])
        scores = jnp.dot(q_scr[slot, rs, pair], keys, preferred_element_type=F32)
        mask = jnp.concatenate([causal, causal], axis=1)
        sd_scr[slot, rs, pair] = jnp.where(mask, scores, 0.0).astype(BF16)

    def retention_summaries(slot, h):
        hl = head_lanes(h)
        values = block_diagonal(v_scr[slot, chunk_rows(0), hl], v_scr[slot, chunk_rows(1), hl])
        kv = jnp.dot(kt_scr[hl, :], values, preferred_element_type=F32)
        kv_scr[slot, 0, h] = kv[:, :HEAD_DIM]
        kv_scr[slot, 1, h] = kv[:, HEAD_DIM:]

    def projection_stage(slot, rot, tail_is_zero):
        piece = lambda p: functools.partial(in_piece, slot, p, rot, tail_is_zero)
        heads = lambda part: range(part * HEADS_PER_PIECE, (part + 1) * HEADS_PER_PIECE)
        prods = lambda part: (
            [functools.partial(retention_scores, slot, h // 2, c)
             for h in heads(part)[::2] for c in range(CHUNKS_PER_HALF)]
            + [functools.partial(retention_summaries, slot, h) for h in heads(part)])
        order = []
        for part in range(PIECES_PER_GROUP):
            order += [piece(g * PIECES_PER_GROUP + part) for g in (Q_GROUP, K_GROUP, V_GROUP)]
            order += prods(part)
        order += [piece(g * PIECES_PER_GROUP + part) for g in (GRET_GROUP, XLRU_GROUP, GLRU_GROUP)
                  for part in range(PIECES_PER_GROUP)]
        return order

    def retention_unit(slot, h, c):
        hl, rs = head_lanes(h), chunk_rows(c)
        state = state_scr[h]
        lhs = jnp.concatenate([sd_scr[slot, rs, hl], q_scr[slot, rs, hl]], axis=1)
        rhs = jnp.concatenate([v_scr[slot, rs, hl], state.astype(BF16)], axis=0)
        o = jnp.dot(lhs, rhs, preferred_element_type=F32) * xi_ref[:, hl]
        state_scr[h] = gdec_ref[:, hl] * (state + kv_scr[slot, c, h])
        mu = jnp.mean(o, axis=-1, keepdims=True)
        oc = o - mu
        var = jnp.mean(oc * oc, axis=-1, keepdims=True)
        gate = gret_scr[slot, rs, hl]
        y = oc * lax.rsqrt(var + NORM_EPS) * (gate * _sigmoid(gate))
        mixed_scr[slot, rs, hl] = y.astype(BF16)

    def lru_unit(slot, n, rh):
        sl = slice(n * LRU_BLOCK_DIM, (n + 1) * LRU_BLOCK_DIM)
        rows = slice(rh * LRU_ROWS, (rh + 1) * LRU_ROWS)
        xc = xc_scr[slot, rows, sl]
        pre = jnp.dot(xcb_scr[slot, rows, sl], wg_ref[n], preferred_element_type=F32)
        z = -lam_ref[:, sl]
        softplus = jnp.maximum(z, 0.0) + jnp.log1p(jnp.exp(-jnp.abs(z)))
        r = _sigmoid(pre[:, :LRU_BLOCK_DIM] + ba_ref[:, sl])
        i = _sigmoid(pre[:, LRU_BLOCK_DIM:] + bx_ref[:, sl])
        a = jnp.exp2(r * ((-LRU_C * LOG2_E) * softplus))
        v = 1.0 - a * a
        mult = jnp.where(v > 0.0, v * lax.rsqrt(v), 0.0)
        b = mult * (i * xc)
        hseq, hcarry_scr[:, sl] = _linear_scan_interleaved(a, b, hcarry_scr[:, sl])
        gate = glru_scr[slot, rows, sl]
        y = hseq * (gate * _sigmoid(gate))
        for g in range(LRU_ROWS // GROUP_ROWS):
            for i in range(RUN):
                src = slice(g * GROUP_ROWS + i * V7X_SUBLANES, g * GROUP_ROWS + (i + 1) * V7X_SUBLANES)
                dst = pl.ds(rows.start + g * GROUP_ROWS + i, V7X_SUBLANES, stride=RUN)
                yslab_scr[slot, n, dst, :] = y[src]

    def pack_lru_outputs(half):
        for n in range(LRU_BLOCKS):
            osl = slice(D_MODEL + n * LRU_BLOCK_DIM, D_MODEL + (n + 1) * LRU_BLOCK_DIM)
            mixed_scr[half, :, osl] = yslab_scr[half, n].astype(BF16)

    def mixer_stage(slot):
        lru = [functools.partial(lru_unit, slot, n, rh)
               for n in range(LRU_BLOCKS) for rh in range(HALF_BLOCK // LRU_ROWS)]
        ret = [functools.partial(retention_unit, slot, h, c)
               for h in range(RET_HEADS) for c in range(CHUNKS_PER_HALF)]
        return _interleave(lru, ret)

    def out_piece(half, p):
        cols = slice(p * MXU_PIECE_COLS, (p + 1) * MXU_PIECE_COLS)
        resid = x_ref[0, 0:hb, cols] if half == 0 else xstash_scr[:, cols]
        hres_scr[:, cols] = resid + jnp.dot(mixed_scr[half], wout_scr[:, cols],
                                            preferred_element_type=F32)

    def norm_output(half):
        hres = hres_scr[...]
        ms = jnp.mean(hres * hres, axis=-1, keepdims=True)
        y = hres * lax.rsqrt(ms + NORM_EPS) * gout_ref[...]
        if half == 0:
            out_stage[...] = y
        else:
            out_ref[0, hb:2 * hb, :] = y

    def output_stage(half):
        return [functools.partial(pack_lru_outputs, half)] + [
            functools.partial(out_piece, half, p) for p in range(PIECES_PER_GROUP)] + [
            functools.partial(norm_output, half)]

    @pl.when(step < n_steps)
    def _():
        @pl.when(step == 0)
        def _():
            stages = [buf.at[slot] for buf in (gret_scr, xc_scr, glru_scr) for slot in range(2)]
            _load_weights_as_bf16([(win_hbm, win_scr), (wout_hbm, wout_scr)], stages, weight_sems)
            norm_input(x_ref[0, 0:hb, :], 0)
            tail_scr[...] = jnp.zeros_like(tail_scr)
            rot = (cos_ref[0:hb, :], sin_ref[0:hb, :], cost_ref[:, 0:hb], sint_ref[:, 0:hb])
            for thunk in projection_stage(0, rot, None):
                thunk()
            mixed_scr[1] = jnp.zeros(mixed_scr.shape[1:], BF16)
            yslab_scr[1] = jnp.zeros(yslab_scr.shape[1:], F32)
            xstash_scr[...] = jnp.zeros_like(xstash_scr)
            out_stage[...] = jnp.zeros_like(out_stage)

        @pl.when(step % steps_per_row == 0)
        def _():
            state_scr[...] = jnp.zeros_like(state_scr)
            hcarry_scr[...] = jnp.zeros_like(hcarry_scr)

        out_ref[0, 0:hb, :] = out_stage[...]
        norm_input(x_ref[0, hb:2 * hb, :], 1)
        rot = (cos_ref[hb:2 * hb, :], sin_ref[hb:2 * hb, :], cost_ref[:, hb:2 * hb], sint_ref[:, hb:2 * hb])
        matrix = output_stage(1) + projection_stage(1, rot, None)
        for thunk in _interleave(mixer_stage(0), matrix):
            thunk()
        xstash_scr[...] = x_ref[0, hb:2 * hb, :]
        norm_input(xnext_ref[0], 0)
        next_starts_row = (step + 1) % steps_per_row == 0
        rot = (cosn_ref[...], sinn_ref[...], cosnt_ref[...], sinnt_ref[...])
        matrix = output_stage(0) + projection_stage(0, rot, next_starts_row)
        for thunk in _interleave(mixer_stage(1), matrix):
            thunk()

    @pl.when(step == n_steps)
    def _():
        out_ref[0, 0:hb, :] = out_stage[...]
        for thunk in output_stage(1):
            thunk()


def _position_tables(seq_len):
    half = np.arange(0, HEAD_DIM, 2, dtype=np.float64)
    inv_freq = ROPE_BASE ** (-half / HEAD_DIM)
    ang = np.arange(seq_len, dtype=np.float64)[:, None] * inv_freq[None, :]
    cos, sin = np.cos(ang), np.sin(ang)
    cos_t = np.concatenate([cos, cos], axis=-1).astype(np.float32)
    sin_t = np.concatenate([-sin, sin], axis=-1).astype(np.float32)
    return cos_t, sin_t


def _decay_tables():
    c = RET_CHUNK
    scale = HEAD_DIM ** -0.5
    log_g = np.log1p(-np.exp2(-5.0 - np.arange(RET_HEADS, dtype=np.float64)))
    idx = np.arange(c, dtype=np.float64)
    xi = np.exp((idx + 1)[None, :] * log_g[:, None])
    kscale = scale * np.exp(-(idx + 1)[None, :] * log_g[:, None])
    gdec = np.exp(c * log_g)

    def rows_by_head_lanes(t):
        return np.repeat(t.T[:, :, None], HEAD_DIM, axis=2).reshape(c, RET_HEADS * HEAD_DIM)

    f32 = lambda t: t.astype(np.float32)
    return f32(kscale), f32(rows_by_head_lanes(xi)), f32(np.repeat(gdec, HEAD_DIM)[None, :])


def _resident(arr):
    nd = arr.ndim
    return pl.BlockSpec(arr.shape, lambda i: (0,) * nd, pipeline_mode=pl.Buffered(1))


_IN_HBM = pl.BlockSpec(memory_space=pl.ANY)


@jax.jit
def kernel(x, norm_in_g, w_in, conv_w, conv_b, gate_a_w, gate_a_b, gate_x_w, gate_x_b,
           lru_lambda, w_out, norm_out_g):
    batch, seq_len, d_model = x.shape
    assert d_model == D_MODEL and seq_len % STEP_BLOCK == 0
    assert w_in.shape == (D_MODEL, IN_WIDTH) and w_out.shape == (MIX_WIDTH, D_MODEL)
    steps_per_row = seq_len // STEP_BLOCK
    n_steps = batch * steps_per_row
    halves_per_row = 2 * steps_per_row

    cos_t, sin_t = _position_tables(seq_len)
    kscale_t, xi_t, gdec_t = _decay_tables()
    w_gates = jnp.concatenate([gate_a_w, gate_x_w], axis=-1).astype(BF16)
    row = lambda p: p.reshape(1, D_MODEL).astype(F32)

    def next_half(i):
        n = jnp.minimum(2 * (i + 1), 2 * n_steps - 2)
        return (n // halves_per_row, n % halves_per_row, 0)

    def block_of_step(i):
        j = jnp.minimum(i, n_steps - 1)
        return (j // steps_per_row, j % steps_per_row, 0)

    def block_of_previous_step(i):
        j = jnp.maximum(i - 1, 0)
        return (j // steps_per_row, j % steps_per_row, 0)

    step_rows = pl.BlockSpec((1, STEP_BLOCK, D_MODEL), block_of_step)
    step_table = pl.BlockSpec((STEP_BLOCK, HEAD_DIM), lambda i: block_of_step(i)[1:])
    next_table = pl.BlockSpec((HALF_BLOCK, HEAD_DIM), lambda i: next_half(i)[1:])
    step_table_ft = pl.BlockSpec((HEAD_DIM, STEP_BLOCK), lambda i: (0, block_of_step(i)[1]))
    next_table_ft = pl.BlockSpec((HEAD_DIM, HALF_BLOCK), lambda i: (0, next_half(i)[1]))
    cos_ft, sin_ft = np.ascontiguousarray(cos_t.T), np.ascontiguousarray(sin_t.T)
    small = (row(norm_in_g), conv_w.astype(F32), row(conv_b), w_gates, row(gate_a_b),
             row(gate_x_b), row(lru_lambda), row(norm_out_g), kscale_t, xi_t, gdec_t)
    (gin, convw, convb, wg, ba, bx, lam, gout, kscale_t, xi_t, gdec_t) = small
    operands_and_specs = (
        (x, step_rows), (x, pl.BlockSpec((1, HALF_BLOCK, D_MODEL), next_half)),
        (cos_t, step_table), (sin_t, step_table), (cos_t, next_table), (sin_t, next_table),
        (cos_ft, step_table_ft), (sin_ft, step_table_ft), (cos_ft, next_table_ft), (sin_ft, next_table_ft),
        (gin, _resident(gin)), (w_in.astype(F32), _IN_HBM), (convw, _resident(convw)),
        (convb, _resident(convb)), (wg, _resident(wg)), (ba, _resident(ba)), (bx, _resident(bx)),
        (lam, _resident(lam)), (w_out.astype(F32), _IN_HBM), (gout, _resident(gout)),
        (kscale_t, _resident(kscale_t)), (xi_t, _resident(xi_t)), (gdec_t, _resident(gdec_t)),
    )
    operands = [op for op, _ in operands_and_specs]
    in_specs = [spec for _, spec in operands_and_specs]

    half_f32 = pltpu.VMEM((2, HALF_BLOCK, D_MODEL), F32)
    half_bf16 = pltpu.VMEM((2, HALF_BLOCK, D_MODEL), BF16)
    scratch_shapes = [
        pltpu.VMEM((D_MODEL, IN_WIDTH), BF16),
        pltpu.VMEM((MIX_WIDTH, D_MODEL), BF16),
        pltpu.SemaphoreType.DMA((WEIGHT_STAGES,)),
        half_bf16,
        pltpu.VMEM((D_MODEL, HALF_BLOCK), BF16),
        half_bf16,
        half_bf16,
        pltpu.VMEM((2, CHUNKS_PER_HALF, RET_HEADS, HEAD_DIM, HEAD_DIM), F32),
        half_f32,
        half_f32,
        half_bf16,
        half_f32,
        half_bf16,
        half_bf16,
        pltpu.VMEM((D_MODEL // LRU_BLOCK_DIM, HALF_BLOCK, LRU_BLOCK_DIM), F32),
        pltpu.VMEM((2, LRU_BLOCKS, HALF_BLOCK, LRU_BLOCK_DIM), F32),
        pltpu.VMEM((HALF_BLOCK, D_MODEL), F32),
        pltpu.VMEM((RET_HEADS, HEAD_DIM, HEAD_DIM), F32),
        pltpu.VMEM((1, D_MODEL), F32),
        pltpu.VMEM((CONV_WIDTH - 1, V7X_SUBLANES, D_MODEL), F32),
        pltpu.VMEM((2, HALF_BLOCK, MIX_WIDTH), BF16),
        pltpu.VMEM((HALF_BLOCK, D_MODEL), F32),
        pltpu.VMEM((HALF_BLOCK, D_MODEL), F32),
    ]

    nbytes = lambda shape, dtype: int(np.prod(shape)) * jnp.dtype(dtype).itemsize
    f32_rows = lambda rows, width: nbytes((rows, width), F32)
    windows = 2 * (f32_rows(2 * STEP_BLOCK + HALF_BLOCK, D_MODEL)
                   + 4 * f32_rows(STEP_BLOCK + HALF_BLOCK, HEAD_DIM))
    vmem_bytes = (sum(nbytes(s.shape, s.dtype) for s in scratch_shapes if s.memory_space == pltpu.VMEM)
                  + windows + sum(nbytes(a.shape, a.dtype) for a in small) + COMPILER_SCRATCH_BYTES)
    assert vmem_bytes <= V7X_VMEM_BYTES, vmem_bytes

    return pl.pallas_call(
        functools.partial(_layer_body, steps_per_row=steps_per_row, n_steps=n_steps),
        grid=(n_steps + 1,),
        in_specs=in_specs,
        out_specs=pl.BlockSpec((1, STEP_BLOCK, D_MODEL), block_of_previous_step),
        out_shape=jax.ShapeDtypeStruct(x.shape, x.dtype),
        scratch_shapes=scratch_shapes,
        compiler_params=pltpu.CompilerParams(
            dimension_semantics=("arbitrary",),
            vmem_limit_bytes=vmem_bytes,
        ),
        name="hybrid_layer",
    )(*operands)
```

```python
import functools

import jax
import jax.numpy as jnp
import numpy as np
from jax import lax
from jax.experimental import pallas as pl
from jax.experimental.pallas import tpu as pltpu

D_MODEL = 1024
RET_HEADS = 8
HEAD_DIM = 128
RET_CHUNK = 128
ROPE_BASE = 10000.0
LRU_BLOCKS = 8
LRU_BLOCK_DIM = 128
LRU_C = 8.0
CONV_WIDTH = 4
MIX_WIDTH = 2 * D_MODEL
IN_WIDTH = 6 * D_MODEL
NORM_EPS = 1e-6

V7X_SUBLANES = 8
HALF_BLOCK = 256
STEP_BLOCK = 2 * HALF_BLOCK
CHUNKS_PER_HALF = HALF_BLOCK // RET_CHUNK
assert CHUNKS_PER_HALF == 2 and RET_HEADS % 2 == 0
MXU_PIECE_COLS = 512
HEADS_PER_PIECE = MXU_PIECE_COLS // HEAD_DIM
PIECES_PER_GROUP = D_MODEL // MXU_PIECE_COLS
RUN = V7X_SUBLANES
GROUP_ROWS = RUN * V7X_SUBLANES
GROUPS_PER_HALF = HALF_BLOCK // GROUP_ROWS
LRU_ROWS = GROUP_ROWS
WEIGHT_BLOCK_ROWS = HALF_BLOCK
WEIGHT_STAGES = 6
V7X_VMEM_BYTES = 64 * 1024 * 1024
COMPILER_SCRATCH_BYTES = 8 * 1024 * 1024

Q_GROUP, K_GROUP, V_GROUP, GRET_GROUP, XLRU_GROUP, GLRU_GROUP = range(6)

F32 = jnp.float32
BF16 = jnp.bfloat16
LOG2_E = 1.4426950408889634


def _sigmoid(z):
    return 1.0 / (1.0 + jnp.exp2(z * (-LOG2_E)))


def _rotate_pair(lo, hi, cos, sin):
    return lo * cos - hi * sin, lo * sin + hi * cos


def _linear_scan_interleaved(a, b, h0):
    lanes = a.shape[1]
    n_groups = a.shape[0] // GROUP_ROWS
    a4 = a.reshape(n_groups, RUN, V7X_SUBLANES, lanes)
    b4 = b.reshape(n_groups, RUN, V7X_SUBLANES, lanes)
    sub = lax.broadcasted_iota(jnp.int32, (V7X_SUBLANES, lanes), 0)
    carry = jnp.broadcast_to(h0, (V7X_SUBLANES, lanes))
    out = []
    for g in range(n_groups):
        h_loc, a_cum = [b4[g, 0]], [a4[g, 0]]
        for i in range(1, RUN):
            h_loc.append(a4[g, i] * h_loc[-1] + b4[g, i])
            a_cum.append(a4[g, i] * a_cum[-1])
        p, e = a_cum[-1], h_loc[-1]
        shift = 1
        while shift < V7X_SUBLANES:
            keep = sub >= shift
            p_prev = jnp.where(keep, pltpu.roll(p, shift, 0), 1.0)
            e_prev = jnp.where(keep, pltpu.roll(e, shift, 0), 0.0)
            e = p * e_prev + e
            p = p * p_prev
            shift *= 2
        first = sub >= 1
        entry = (jnp.where(first, pltpu.roll(e, 1, 0), 0.0)
                 + jnp.where(first, pltpu.roll(p, 1, 0), 1.0) * carry)
        hs = [h_loc[i] + a_cum[i] * entry for i in range(RUN)]
        out += hs
        last = hs[-1][V7X_SUBLANES - 1:V7X_SUBLANES, :]
        carry = jnp.broadcast_to(last, (V7X_SUBLANES, lanes))
    return jnp.concatenate(out, axis=0), last


def _pair_layout(block):
    half = HEAD_DIM // 2
    parts = []
    for base in range(0, block.shape[1], 2 * HEAD_DIM):
        for start in (0, HEAD_DIM, half, HEAD_DIM + half):
            parts.append(block[:, base + start:base + start + half])
    return jnp.concatenate(parts, axis=1)


def _load_weights_as_bf16(weights, stages, sems):
    blocks = [(w_hbm, w_scr, r, c, c in paired)
              for w_hbm, w_scr, paired in weights
              for r in range(w_hbm.shape[0] // WEIGHT_BLOCK_ROWS)
              for c in range(w_hbm.shape[1] // D_MODEL)]
    n_stage = len(stages)

    def block_copy(k):
        w_hbm, _, r, c, _ = blocks[k]
        src = w_hbm.at[pl.ds(r * WEIGHT_BLOCK_ROWS, WEIGHT_BLOCK_ROWS), pl.ds(c * D_MODEL, D_MODEL)]
        return pltpu.make_async_copy(src, stages[k % n_stage], sems.at[k % n_stage])

    for k in range(min(n_stage, len(blocks))):
        block_copy(k).start()
    for k, (_, w_scr, r, c, paired) in enumerate(blocks):
        block_copy(k).wait()
        block = stages[k % n_stage][...]
        w_scr[r * WEIGHT_BLOCK_ROWS:(r + 1) * WEIGHT_BLOCK_ROWS, c * D_MODEL:(c + 1) * D_MODEL] = (
            (_pair_layout(block) if paired else block).astype(BF16))
        if k + n_stage < len(blocks):
            block_copy(k + n_stage).start()


def _interleave(vector_units, matrix_units):
    order, i, j = [], 0, 0
    nv, nm = max(len(vector_units), 1), max(len(matrix_units), 1)
    while i < len(vector_units) or j < len(matrix_units):
        if j < len(matrix_units) and (i >= len(vector_units) or j * nv <= i * nm):
            order.append(matrix_units[j])
            j += 1
        else:
            order.append(vector_units[i])
            i += 1
    return order


def _layer_body(x_ref, xnext_ref, cos_ref, sin_ref, cosn_ref, sinn_ref, cost_ref, sint_ref,
                cosnt_ref, sinnt_ref, gin_ref, win_hbm,
                convw_ref, convb_ref, wg_ref, ba_ref, bx_ref, lam_ref, wout_hbm, gout_ref,
                kscale_ref, xi_ref, gdec_ref, out_ref,
                win_scr, wout_scr, weight_sems,
                q_scr, kt_scr, v_scr, sd_scr, kv_scr, gret_scr, xc_scr, xcb_scr, glru_scr,
                xn_scr, xnp_scr, xslab_scr, yslab_scr, hres_scr, state_scr, hcarry_scr, tail_scr,
                mixed_scr, xstash_scr, out_stage, *, steps_per_row, n_steps):
    hb = HALF_BLOCK
    step = pl.program_id(0)
    causal = (lax.broadcasted_iota(jnp.int32, (RET_CHUNK, RET_CHUNK), 0)
              >= lax.broadcasted_iota(jnp.int32, (RET_CHUNK, RET_CHUNK), 1))

    def head_lanes(h):
        return slice(h * HEAD_DIM, (h + 1) * HEAD_DIM)

    def chunk_rows(c):
        return slice(c * RET_CHUNK, (c + 1) * RET_CHUNK)

    def norm_input(x, slot):
        ms = jnp.mean(x * x, axis=-1, keepdims=True)
        inv_rms = lax.rsqrt(ms + NORM_EPS)
        for nb in range(D_MODEL // LRU_BLOCK_DIM):
            sl = slice(nb * LRU_BLOCK_DIM, (nb + 1) * LRU_BLOCK_DIM)
            xn = x[:, sl] * inv_rms * gin_ref[:, sl]
            xn_scr[slot, :, sl] = xn.astype(BF16)
            xslab_scr[nb] = xn
        pieces = []
        for g in range(GROUPS_PER_HALF):
            for i in range(RUN):
                rows = pl.ds(g * GROUP_ROWS + i, V7X_SUBLANES, stride=RUN)
                pieces.append(jnp.concatenate(
                    [xslab_scr[nb, rows, :] for nb in range(D_MODEL // LRU_BLOCK_DIM)], axis=1))
        xnp_scr[slot] = jnp.concatenate(pieces, axis=0).astype(BF16)

    def in_piece(slot, p, rot, tail_is_zero):
        group, part = divmod(p, PIECES_PER_GROUP)
        cols = slice(p * MXU_PIECE_COLS, (p + 1) * MXU_PIECE_COLS)
        lanes = slice(part * MXU_PIECE_COLS, (part + 1) * MXU_PIECE_COLS)
        lhs = xnp_scr[slot] if group in (XLRU_GROUP, GLRU_GROUP) else xn_scr[slot]
        val = jnp.dot(lhs, win_scr[:, cols], preferred_element_type=F32)
        cos_t, sin_t, cos_ft, sin_ft = rot
        pair_w = 2 * HEAD_DIM
        if group == Q_GROUP:
            for j in range(MXU_PIECE_COLS // pair_w):
                lo, hi = _rotate_pair(val[:, j * pair_w:j * pair_w + HEAD_DIM],
                                      val[:, j * pair_w + HEAD_DIM:(j + 1) * pair_w], cos_t, sin_t)
                gp = part * MXU_PIECE_COLS + j * pair_w
                q_scr[slot, :, gp:gp + pair_w] = jnp.concatenate([lo, hi], axis=1).astype(BF16)
        elif group == K_GROUP:
            for j in range(MXU_PIECE_COLS // pair_w):
                hp = (part * MXU_PIECE_COLS + j * pair_w) // pair_w
                for c in range(CHUNKS_PER_HALF):
                    rs = chunk_rows(c)
                    lo, hi = _rotate_pair(val[rs, j * pair_w:j * pair_w + HEAD_DIM].T,
                                          val[rs, j * pair_w + HEAD_DIM:(j + 1) * pair_w].T,
                                          cos_ft[:, rs], sin_ft[:, rs])
                    kt_scr[hp * pair_w:hp * pair_w + HEAD_DIM, rs] = (lo * kscale_ref[hp]).astype(BF16)
                    kt_scr[hp * pair_w + HEAD_DIM:(hp + 1) * pair_w, rs] = (hi * kscale_ref[hp]).astype(BF16)
        elif group == V_GROUP:
            v_scr[slot, :, lanes] = val.astype(BF16)
        elif group == GRET_GROUP:
            gret_scr[slot, :, lanes] = val
        elif group == GLRU_GROUP:
            glru_scr[slot, :, lanes] = val
        else:
            taps = CONV_WIDTH - 1
            x4 = val.reshape(GROUPS_PER_HALF, RUN, V7X_SUBLANES, MXU_PIECE_COLS)
            prev_tail = tail_scr[:, :, lanes]
            if tail_is_zero is not None:
                prev_tail = jnp.where(tail_is_zero, 0.0, prev_tail)
            tail_scr[:, :, lanes] = x4[GROUPS_PER_HALF - 1, RUN - taps:RUN]
            late = x4[:, RUN - taps:RUN]
            late_prev = jnp.concatenate([prev_tail[None], late[:-1]], axis=0)
            flat = (GROUPS_PER_HALF * taps, V7X_SUBLANES, MXU_PIECE_COLS)
            sub = lax.broadcasted_iota(jnp.int32, flat, 1)
            wrapped = jnp.where(sub >= 1, pltpu.roll(late.reshape(flat), 1, 1),
                                pltpu.roll(late_prev.reshape(flat), 1, 1)).reshape(late.shape)
            xc4 = convb_ref[:, lanes] + x4 * convw_ref[taps:CONV_WIDTH, lanes]
            for s in range(1, CONV_WIDTH):
                src = jnp.concatenate([wrapped[:, taps - s:], x4[:, :RUN - s]], axis=1)
                xc4 = xc4 + src * convw_ref[taps - s:CONV_WIDTH - s, lanes]
            xc = xc4.reshape(hb, MXU_PIECE_COLS)
            xc_scr[slot, :, lanes] = xc
            xcb_scr[slot, :, lanes] = xc.astype(BF16)

    def block_diagonal(top_left, bottom_right):
        zeros = jnp.zeros_like(top_left)
        return jnp.concatenate([jnp.concatenate([top_left, zeros], axis=1),
                                jnp.concatenate([zeros, bottom_right], axis=1)], axis=0)

    half_dim = HEAD_DIM // 2

    def pair_rows(top, bottom):
        return jnp.concatenate([block_diagonal(top[:half_dim], top[half_dim:]),
                                block_diagonal(bottom[:half_dim], bottom[half_dim:])], axis=0)

    def retention_scores(slot, hp, c):
        rs = chunk_rows(c)
        pair = slice(2 * hp * HEAD_DIM, (2 * hp + 2) * HEAD_DIM)
        keys = pair_rows(kt_scr[pair.start:pair.start + HEAD_DIM, rs],
                         kt_scr[pair.start + HEAD_DIM:pair.stop, rs])
        scores = jnp.dot(q_scr[slot, rs, pair], keys, preferred_element_type=F32)
        mask = jnp.concatenate([causal, causal], axis=1)
        sd_scr[slot, rs, pair] = jnp.where(mask, scores, 0.0).astype(BF16)

    def retention_summaries(slot, h):
        hl = head_lanes(h)
        base = (h // 2) * 2 * HEAD_DIM + (h % 2) * half_dim
        kt = jnp.concatenate([kt_scr[base:base + half_dim, :],
                              kt_scr[base + HEAD_DIM:base + HEAD_DIM + half_dim, :]], axis=0)
        values = block_diagonal(v_scr[slot, chunk_rows(0), hl], v_scr[slot, chunk_rows(1), hl])
        kv = jnp.dot(kt, values, preferred_element_type=F32)
        kv_scr[slot, 0, h] = kv[:, :HEAD_DIM]
        kv_scr[slot, 1, h] = kv[:, HEAD_DIM:]

    def projection_stage(slot, rot, tail_is_zero):
        piece = lambda p: functools.partial(in_piece, slot, p, rot, tail_is_zero)
        heads = lambda part: range(part * HEADS_PER_PIECE, (part + 1) * HEADS_PER_PIECE)
        prods = lambda part: (
            [functools.partial(retention_scores, slot, h // 2, c)
             for h in heads(part)[::2] for c in range(CHUNKS_PER_HALF)]
            + [functools.partial(retention_summaries, slot, h) for h in heads(part)])
        order = []
        for part in range(PIECES_PER_GROUP):
            order += [piece(g * PIECES_PER_GROUP + part) for g in (Q_GROUP, K_GROUP, V_GROUP)]
            order += prods(part)
        order += [piece(g * PIECES_PER_GROUP + part) for g in (GRET_GROUP, XLRU_GROUP, GLRU_GROUP)
                  for part in range(PIECES_PER_GROUP)]
        return order

    def retention_unit(slot, hp, c):
        rs = chunk_rows(c)
        h0, h1 = 2 * hp, 2 * hp + 1
        pair = slice(h0 * HEAD_DIM, (h1 + 1) * HEAD_DIM)
        s0, s1 = state_scr[h0], state_scr[h1]
        b0, b1 = s0.astype(BF16), s1.astype(BF16)
        lhs = jnp.concatenate([sd_scr[slot, rs, pair], q_scr[slot, rs, pair]], axis=1)
        rhs = jnp.concatenate([
            block_diagonal(v_scr[slot, rs, head_lanes(h0)], v_scr[slot, rs, head_lanes(h1)]),
            pair_rows(jnp.concatenate([b0[:half_dim], b1[:half_dim]], axis=0),
                      jnp.concatenate([b0[half_dim:], b1[half_dim:]], axis=0))], axis=0)
        o2 = jnp.dot(lhs, rhs, preferred_element_type=F32)
        for h, state, lanes in ((h0, s0, slice(0, HEAD_DIM)), (h1, s1, slice(HEAD_DIM, 2 * HEAD_DIM))):
            hl = head_lanes(h)
            o = o2[:, lanes] * xi_ref[:, hl]
            state_scr[h] = gdec_ref[:, hl] * (state + kv_scr[slot, c, h])
            mu = jnp.mean(o, axis=-1, keepdims=True)
            oc = o - mu
            var = jnp.mean(oc * oc, axis=-1, keepdims=True)
            gate = gret_scr[slot, rs, hl]
            y = oc * lax.rsqrt(var + NORM_EPS) * (gate * _sigmoid(gate))
            mixed_scr[slot, rs, hl] = y.astype(BF16)

    def lru_unit(slot, n, rh):
        sl = slice(n * LRU_BLOCK_DIM, (n + 1) * LRU_BLOCK_DIM)
        rows = slice(rh * LRU_ROWS, (rh + 1) * LRU_ROWS)
        xc = xc_scr[slot, rows, sl]
        pre = jnp.dot(xcb_scr[slot, rows, sl], wg_ref[n], preferred_element_type=F32)
        z = -lam_ref[:, sl]
        softplus = jnp.maximum(z, 0.0) + jnp.log1p(jnp.exp(-jnp.abs(z)))
        r = _sigmoid(pre[:, :LRU_BLOCK_DIM] + ba_ref[:, sl])
        i = _sigmoid(pre[:, LRU_BLOCK_DIM:] + bx_ref[:, sl])
        a = jnp.exp2(r * ((-LRU_C * LOG2_E) * softplus))
        v = 1.0 - a * a
        mult = jnp.where(v > 0.0, v * lax.rsqrt(v), 0.0)
        b = mult * (i * xc)
        hseq, hcarry_scr[:, sl] = _linear_scan_interleaved(a, b, hcarry_scr[:, sl])
        gate = glru_scr[slot, rows, sl]
        y = hseq * (gate * _sigmoid(gate))
        for g in range(LRU_ROWS // GROUP_ROWS):
            for i in range(RUN):
                src = slice(g * GROUP_ROWS + i * V7X_SUBLANES, g * GROUP_ROWS + (i + 1) * V7X_SUBLANES)
                dst = pl.ds(rows.start + g * GROUP_ROWS + i, V7X_SUBLANES, stride=RUN)
                yslab_scr[slot, n, dst, :] = y[src]

    def pack_lru_outputs(half):
        for n in range(LRU_BLOCKS):
            osl = slice(D_MODEL + n * LRU_BLOCK_DIM, D_MODEL + (n + 1) * LRU_BLOCK_DIM)
            mixed_scr[half, :, osl] = yslab_scr[half, n].astype(BF16)

    def mixer_stage(slot):
        lru = [functools.partial(lru_unit, slot, n, rh)
               for n in range(LRU_BLOCKS) for rh in range(HALF_BLOCK // LRU_ROWS)]
        ret = [functools.partial(retention_unit, slot, hp, c)
               for hp in range(RET_HEADS // 2) for c in range(CHUNKS_PER_HALF)]
        return _interleave(lru, ret)

    def out_piece(half, p):
        cols = slice(p * MXU_PIECE_COLS, (p + 1) * MXU_PIECE_COLS)
        resid = x_ref[0, 0:hb, cols] if half == 0 else xstash_scr[:, cols]
        hres_scr[:, cols] = resid + jnp.dot(mixed_scr[half], wout_scr[:, cols],
                                            preferred_element_type=F32)

    def norm_output(half):
        hres = hres_scr[...]
        ms = jnp.mean(hres * hres, axis=-1, keepdims=True)
        y = hres * lax.rsqrt(ms + NORM_EPS) * gout_ref[...]
        if half == 0:
            out_stage[...] = y
        else:
            out_ref[0, hb:2 * hb, :] = y

    def output_stage(half):
        return [functools.partial(pack_lru_outputs, half)] + [
            functools.partial(out_piece, half, p) for p in range(PIECES_PER_GROUP)] + [
            functools.partial(norm_output, half)]

    @pl.when(step < n_steps)
    def _():
        @pl.when(step == 0)
        def _():
            stages = [buf.at[slot] for buf in (gret_scr, xc_scr, glru_scr) for slot in range(2)]
            _load_weights_as_bf16([(win_hbm, win_scr, (Q_GROUP, K_GROUP)), (wout_hbm, wout_scr, ())],
                                  stages, weight_sems)
            norm_input(x_ref[0, 0:hb, :], 0)
            tail_scr[...] = jnp.zeros_like(tail_scr)
            rot = (cos_ref[0:hb, :], sin_ref[0:hb, :], cost_ref[:, 0:hb], sint_ref[:, 0:hb])
            for thunk in projection_stage(0, rot, None):
                thunk()
            mixed_scr[1] = jnp.zeros(mixed_scr.shape[1:], BF16)
            yslab_scr[1] = jnp.zeros(yslab_scr.shape[1:], F32)
            xstash_scr[...] = jnp.zeros_like(xstash_scr)
            out_stage[...] = jnp.zeros_like(out_stage)

        @pl.when(step % steps_per_row == 0)
        def _():
            state_scr[...] = jnp.zeros_like(state_scr)
            hcarry_scr[...] = jnp.zeros_like(hcarry_scr)

        out_ref[0, 0:hb, :] = out_stage[...]
        norm_input(x_ref[0, hb:2 * hb, :], 1)
        rot = (cos_ref[hb:2 * hb, :], sin_ref[hb:2 * hb, :], cost_ref[:, hb:2 * hb], sint_ref[:, hb:2 * hb])
        matrix = output_stage(1) + projection_stage(1, rot, None)
        for thunk in _interleave(mixer_stage(0), matrix):
            thunk()
        xstash_scr[...] = x_ref[0, hb:2 * hb, :]
        norm_input(xnext_ref[0], 0)
        next_starts_row = (step + 1) % steps_per_row == 0
        rot = (cosn_ref[...], sinn_ref[...], cosnt_ref[...], sinnt_ref[...])
        matrix = output_stage(0) + projection_stage(0, rot, next_starts_row)
        for thunk in _interleave(mixer_stage(1), matrix):
            thunk()

    @pl.when(step == n_steps)
    def _():
        out_ref[0, 0:hb, :] = out_stage[...]
        for thunk in output_stage(1):
            thunk()


def _position_tables(seq_len):
    half = np.arange(0, HEAD_DIM, 2, dtype=np.float64)
    inv_freq = ROPE_BASE ** (-half / HEAD_DIM)
    ang = np.arange(seq_len, dtype=np.float64)[:, None] * inv_freq[None, :]
    cos, sin = np.cos(ang), np.sin(ang)
    cos_t = np.concatenate([cos, cos], axis=-1).astype(np.float32)
    sin_t = np.concatenate([sin, sin], axis=-1).astype(np.float32)
    return cos_t, sin_t


def _decay_tables():
    c = RET_CHUNK
    scale = HEAD_DIM ** -0.5
    log_g = np.log1p(-np.exp2(-5.0 - np.arange(RET_HEADS, dtype=np.float64)))
    idx = np.arange(c, dtype=np.float64)
    xi = np.exp((idx + 1)[None, :] * log_g[:, None])
    kscale = scale * np.exp(-(idx + 1)[None, :] * log_g[:, None])
    gdec = np.exp(c * log_g)

    def rows_by_head_lanes(t):
        return np.repeat(t.T[:, :, None], HEAD_DIM, axis=2).reshape(c, RET_HEADS * HEAD_DIM)

    f32 = lambda t: t.astype(np.float32)
    kscale_pairs = np.repeat(kscale.reshape(RET_HEADS // 2, 2, 1, c), HEAD_DIM // 2, axis=2)
    kscale_pairs = kscale_pairs.reshape(RET_HEADS // 2, HEAD_DIM, c)
    return f32(kscale_pairs), f32(rows_by_head_lanes(xi)), f32(np.repeat(gdec, HEAD_DIM)[None, :])


def _resident(arr):
    nd = arr.ndim
    return pl.BlockSpec(arr.shape, lambda i: (0,) * nd, pipeline_mode=pl.Buffered(1))


_IN_HBM = pl.BlockSpec(memory_space=pl.ANY)


@jax.jit
def kernel(x, norm_in_g, w_in, conv_w, conv_b, gate_a_w, gate_a_b, gate_x_w, gate_x_b,
           lru_lambda, w_out, norm_out_g):
    batch, seq_len, d_model = x.shape
    assert d_model == D_MODEL and seq_len % STEP_BLOCK == 0
    assert w_in.shape == (D_MODEL, IN_WIDTH) and w_out.shape == (MIX_WIDTH, D_MODEL)
    steps_per_row = seq_len // STEP_BLOCK
    n_steps = batch * steps_per_row
    halves_per_row = 2 * steps_per_row

    cos_t, sin_t = _position_tables(seq_len)
    kscale_t, xi_t, gdec_t = _decay_tables()
    w_gates = jnp.concatenate([gate_a_w, gate_x_w], axis=-1).astype(BF16)
    row = lambda p: p.reshape(1, D_MODEL).astype(F32)

    def next_half(i):
        n = jnp.minimum(2 * (i + 1), 2 * n_steps - 2)
        return (n // halves_per_row, n % halves_per_row, 0)

    def block_of_step(i):
        j = jnp.minimum(i, n_steps - 1)
        return (j // steps_per_row, j % steps_per_row, 0)

    def block_of_previous_step(i):
        j = jnp.maximum(i - 1, 0)
        return (j // steps_per_row, j % steps_per_row, 0)

    step_rows = pl.BlockSpec((1, STEP_BLOCK, D_MODEL), block_of_step)
    step_table = pl.BlockSpec((STEP_BLOCK, HEAD_DIM), lambda i: block_of_step(i)[1:])
    next_table = pl.BlockSpec((HALF_BLOCK, HEAD_DIM), lambda i: next_half(i)[1:])
    step_table_ft = pl.BlockSpec((HEAD_DIM, STEP_BLOCK), lambda i: (0, block_of_step(i)[1]))
    next_table_ft = pl.BlockSpec((HEAD_DIM, HALF_BLOCK), lambda i: (0, next_half(i)[1]))
    cos_ft, sin_ft = np.ascontiguousarray(cos_t.T), np.ascontiguousarray(sin_t.T)
    small = (row(norm_in_g), conv_w.astype(F32), row(conv_b), w_gates, row(gate_a_b),
             row(gate_x_b), row(lru_lambda), row(norm_out_g), kscale_t, xi_t, gdec_t)
    (gin, convw, convb, wg, ba, bx, lam, gout, kscale_t, xi_t, gdec_t) = small
    operands_and_specs = (
        (x, step_rows), (x, pl.BlockSpec((1, HALF_BLOCK, D_MODEL), next_half)),
        (cos_t, step_table), (sin_t, step_table), (cos_t, next_table), (sin_t, next_table),
        (cos_ft, step_table_ft), (sin_ft, step_table_ft), (cos_ft, next_table_ft), (sin_ft, next_table_ft),
        (gin, _resident(gin)), (w_in.astype(F32), _IN_HBM), (convw, _resident(convw)),
        (convb, _resident(convb)), (wg, _resident(wg)), (ba, _resident(ba)), (bx, _resident(bx)),
        (lam, _resident(lam)), (w_out.astype(F32), _IN_HBM), (gout, _resident(gout)),
        (kscale_t, _resident(kscale_t)), (xi_t, _resident(xi_t)), (gdec_t, _resident(gdec_t)),
    )
    operands = [op for op, _ in operands_and_specs]
    in_specs = [spec for _, spec in operands_and_specs]

    half_f32 = pltpu.VMEM((2, HALF_BLOCK, D_MODEL), F32)
    half_bf16 = pltpu.VMEM((2, HALF_BLOCK, D_MODEL), BF16)
    scratch_shapes = [
        pltpu.VMEM((D_MODEL, IN_WIDTH), BF16),
        pltpu.VMEM((MIX_WIDTH, D_MODEL), BF16),
        pltpu.SemaphoreType.DMA((WEIGHT_STAGES,)),
        half_bf16,
        pltpu.VMEM((D_MODEL, HALF_BLOCK), BF16),
        half_bf16,
        half_bf16,
        pltpu.VMEM((2, CHUNKS_PER_HALF, RET_HEADS, HEAD_DIM, HEAD_DIM), F32),
        half_f32,
        half_f32,
        half_bf16,
        half_f32,
        half_bf16,
        half_bf16,
        pltpu.VMEM((D_MODEL // LRU_BLOCK_DIM, HALF_BLOCK, LRU_BLOCK_DIM), F32),
        pltpu.VMEM((2, LRU_BLOCKS, HALF_BLOCK, LRU_BLOCK_DIM), F32),
        pltpu.VMEM((HALF_BLOCK, D_MODEL), F32),
        pltpu.VMEM((RET_HEADS, HEAD_DIM, HEAD_DIM), F32),
        pltpu.VMEM((1, D_MODEL), F32),
        pltpu.VMEM((CONV_WIDTH - 1, V7X_SUBLANES, D_MODEL), F32),
        pltpu.VMEM((2, HALF_BLOCK, MIX_WIDTH), BF16),
        pltpu.VMEM((HALF_BLOCK, D_MODEL), F32),
        pltpu.VMEM((HALF_BLOCK, D_MODEL), F32),
    ]

    nbytes = lambda shape, dtype: int(np.prod(shape)) * jnp.dtype(dtype).itemsize
    f32_rows = lambda rows, width: nbytes((rows, width), F32)
    windows = 2 * (f32_rows(2 * STEP_BLOCK + HALF_BLOCK, D_MODEL)
                   + 4 * f32_rows(STEP_BLOCK + HALF_BLOCK, HEAD_DIM))
    vmem_bytes = (sum(nbytes(s.shape, s.dtype) for s in scratch_shapes if s.memory_space == pltpu.VMEM)
                  + windows + sum(nbytes(a.shape, a.dtype) for a in small) + COMPILER_SCRATCH_BYTES)
    assert vmem_bytes <= V7X_VMEM_BYTES, vmem_bytes

    return pl.pallas_call(
        functools.partial(_layer_body, steps_per_row=steps_per_row, n_steps=n_steps),
        grid=(n_steps + 1,),
        in_specs=in_specs,
        out_specs=pl.BlockSpec((1, STEP_BLOCK, D_MODEL), block_of_previous_step),
        out_shape=jax.ShapeDtypeStruct(x.shape, x.dtype),
        scratch_shapes=scratch_shapes,
        compiler_params=pltpu.CompilerParams(
            dimension_semantics=("arbitrary",),
            vmem_limit_bytes=vmem_bytes,
        ),
        name="hybrid_layer",
    )(*operands)
```

```python
import functools

import jax
import jax.numpy as jnp
import numpy as np
from jax import lax
from jax.experimental import pallas as pl
from jax.experimental.pallas import tpu as pltpu

D_MODEL = 1024
RET_HEADS = 8
HEAD_DIM = 128
RET_CHUNK = 128
ROPE_BASE = 10000.0
LRU_BLOCKS = 8
LRU_BLOCK_DIM = 128
LRU_C = 8.0
CONV_WIDTH = 4
MIX_WIDTH = 2 * D_MODEL
IN_WIDTH = 6 * D_MODEL
NORM_EPS = 1e-6

V7X_SUBLANES = 8
HALF_BLOCK = 256
STEP_BLOCK = 2 * HALF_BLOCK
CHUNKS_PER_HALF = HALF_BLOCK // RET_CHUNK
assert CHUNKS_PER_HALF == 2 and RET_HEADS % 2 == 0
MXU_PIECE_COLS = 512
HEADS_PER_PIECE = MXU_PIECE_COLS // HEAD_DIM
PIECES_PER_GROUP = D_MODEL // MXU_PIECE_COLS
RUN = V7X_SUBLANES
GROUP_ROWS = RUN * V7X_SUBLANES
GROUPS_PER_HALF = HALF_BLOCK // GROUP_ROWS
LRU_ROWS = GROUP_ROWS
WEIGHT_BLOCK_ROWS = HALF_BLOCK
WEIGHT_STAGES = 6
X_RING = 4
V7X_VMEM_BYTES = 64 * 1024 * 1024
COMPILER_SCRATCH_BYTES = 8 * 1024 * 1024

Q_GROUP, K_GROUP, V_GROUP, GRET_GROUP, XLRU_GROUP, GLRU_GROUP = range(6)

F32 = jnp.float32
BF16 = jnp.bfloat16
LOG2_E = 1.4426950408889634


def _sigmoid(z):
    return 1.0 / (1.0 + jnp.exp2(z * (-LOG2_E)))


def _rotate_pair(lo, hi, cos, sin):
    return lo * cos - hi * sin, lo * sin + hi * cos


def _linear_scan_interleaved(a, b, h0):
    lanes = a.shape[1]
    n_groups = a.shape[0] // GROUP_ROWS
    a4 = a.reshape(n_groups, RUN, V7X_SUBLANES, lanes)
    b4 = b.reshape(n_groups, RUN, V7X_SUBLANES, lanes)
    sub = lax.broadcasted_iota(jnp.int32, (V7X_SUBLANES, lanes), 0)
    carry = jnp.broadcast_to(h0, (V7X_SUBLANES, lanes))
    out = []
    for g in range(n_groups):
        h_loc, a_cum = [b4[g, 0]], [a4[g, 0]]
        for i in range(1, RUN):
            h_loc.append(a4[g, i] * h_loc[-1] + b4[g, i])
            a_cum.append(a4[g, i] * a_cum[-1])
        p, e = a_cum[-1], h_loc[-1]
        shift = 1
        while shift < V7X_SUBLANES:
            keep = sub >= shift
            p_prev = jnp.where(keep, pltpu.roll(p, shift, 0), 1.0)
            e_prev = jnp.where(keep, pltpu.roll(e, shift, 0), 0.0)
            e = p * e_prev + e
            p = p * p_prev
            shift *= 2
        first = sub >= 1
        entry = (jnp.where(first, pltpu.roll(e, 1, 0), 0.0)
                 + jnp.where(first, pltpu.roll(p, 1, 0), 1.0) * carry)
        hs = [h_loc[i] + a_cum[i] * entry for i in range(RUN)]
        out += hs
        last = hs[-1][V7X_SUBLANES - 1:V7X_SUBLANES, :]
        carry = jnp.broadcast_to(last, (V7X_SUBLANES, lanes))
    return jnp.concatenate(out, axis=0), last


def _pair_layout(block):
    half = HEAD_DIM // 2
    parts = []
    for base in range(0, block.shape[1], 2 * HEAD_DIM):
        for start in (0, HEAD_DIM, half, HEAD_DIM + half):
            parts.append(block[:, base + start:base + start + half])
    return jnp.concatenate(parts, axis=1)


def _load_weights_as_bf16(weights, stages, sems):
    blocks = [(w_hbm, w_scr, r, c, c in paired)
              for w_hbm, w_scr, paired in weights
              for r in range(w_hbm.shape[0] // WEIGHT_BLOCK_ROWS)
              for c in range(w_hbm.shape[1] // D_MODEL)]
    n_stage = len(stages)

    def block_copy(k):
        w_hbm, _, r, c, _ = blocks[k]
        src = w_hbm.at[pl.ds(r * WEIGHT_BLOCK_ROWS, WEIGHT_BLOCK_ROWS), pl.ds(c * D_MODEL, D_MODEL)]
        return pltpu.make_async_copy(src, stages[k % n_stage], sems.at[k % n_stage])

    for k in range(min(n_stage, len(blocks))):
        block_copy(k).start()
    for k, (_, w_scr, r, c, paired) in enumerate(blocks):
        block_copy(k).wait()
        block = stages[k % n_stage][...]
        w_scr[r * WEIGHT_BLOCK_ROWS:(r + 1) * WEIGHT_BLOCK_ROWS, c * D_MODEL:(c + 1) * D_MODEL] = (
            (_pair_layout(block) if paired else block).astype(BF16))
        if k + n_stage < len(blocks):
            block_copy(k + n_stage).start()


def _interleave(vector_units, matrix_units):
    order, i, j = [], 0, 0
    nv, nm = max(len(vector_units), 1), max(len(matrix_units), 1)
    while i < len(vector_units) or j < len(matrix_units):
        if j < len(matrix_units) and (i >= len(vector_units) or j * nv <= i * nm):
            order.append(matrix_units[j])
            j += 1
        else:
            order.append(vector_units[i])
            i += 1
    return order


def _layer_body(x_hbm, cos_ref, sin_ref, cosn_ref, sinn_ref, cost_ref, sint_ref,
                cosnt_ref, sinnt_ref, gin_ref, win_hbm,
                convw_ref, convb_ref, wg_ref, ba_ref, bx_ref, lam_ref, wout_hbm, gout_ref,
                kscale_ref, xi_ref, gdec_ref, out_ref,
                win_scr, wout_scr, weight_sems,
                q_scr, kt_scr, v_scr, sd_scr, kv_scr, gret_scr, xc_scr, xcb_scr, glru_scr,
                xn_scr, xnp_scr, xslab_scr, yslab_scr, hres_scr, state_scr, hcarry_scr, tail_scr,
                mixed_scr, xring_scr, x_sems, out_stage, *, steps_per_row, n_steps):
    hb = HALF_BLOCK
    step = pl.program_id(0)
    causal = (lax.broadcasted_iota(jnp.int32, (RET_CHUNK, RET_CHUNK), 0)
              >= lax.broadcasted_iota(jnp.int32, (RET_CHUNK, RET_CHUNK), 1))

    def head_lanes(h):
        return slice(h * HEAD_DIM, (h + 1) * HEAD_DIM)

    def chunk_rows(c):
        return slice(c * RET_CHUNK, (c + 1) * RET_CHUNK)

    def norm_input(x, slot):
        ms = jnp.mean(x * x, axis=-1, keepdims=True)
        inv_rms = lax.rsqrt(ms + NORM_EPS)
        for nb in range(D_MODEL // LRU_BLOCK_DIM):
            sl = slice(nb * LRU_BLOCK_DIM, (nb + 1) * LRU_BLOCK_DIM)
            xn = x[:, sl] * inv_rms * gin_ref[:, sl]
            xn_scr[slot, :, sl] = xn.astype(BF16)
            xslab_scr[nb] = xn
        pieces = []
        for g in range(GROUPS_PER_HALF):
            for i in range(RUN):
                rows = pl.ds(g * GROUP_ROWS + i, V7X_SUBLANES, stride=RUN)
                pieces.append(jnp.concatenate(
                    [xslab_scr[nb, rows, :] for nb in range(D_MODEL // LRU_BLOCK_DIM)], axis=1))
        xnp_scr[slot] = jnp.concatenate(pieces, axis=0).astype(BF16)

    def in_piece(slot, p, rot, tail_is_zero):
        group, part = divmod(p, PIECES_PER_GROUP)
        cols = slice(p * MXU_PIECE_COLS, (p + 1) * MXU_PIECE_COLS)
        lanes = slice(part * MXU_PIECE_COLS, (part + 1) * MXU_PIECE_COLS)
        lhs = xnp_scr[slot] if group in (XLRU_GROUP, GLRU_GROUP) else xn_scr[slot]
        val = jnp.dot(lhs, win_scr[:, cols], preferred_element_type=F32)
        cos_t, sin_t, cos_ft, sin_ft = rot
        pair_w = 2 * HEAD_DIM
        if group == Q_GROUP:
            for j in range(MXU_PIECE_COLS // pair_w):
                lo, hi = _rotate_pair(val[:, j * pair_w:j * pair_w + HEAD_DIM],
                                      val[:, j * pair_w + HEAD_DIM:(j + 1) * pair_w], cos_t, sin_t)
                gp = part * MXU_PIECE_COLS + j * pair_w
                q_scr[slot, :, gp:gp + pair_w] = jnp.concatenate([lo, hi], axis=1).astype(BF16)
        elif group == K_GROUP:
            for j in range(MXU_PIECE_COLS // pair_w):
                hp = (part * MXU_PIECE_COLS + j * pair_w) // pair_w
                for c in range(CHUNKS_PER_HALF):
                    rs = chunk_rows(c)
                    lo, hi = _rotate_pair(val[rs, j * pair_w:j * pair_w + HEAD_DIM].T,
                                          val[rs, j * pair_w + HEAD_DIM:(j + 1) * pair_w].T,
                                          cos_ft[:, rs], sin_ft[:, rs])
                    kt_scr[hp * pair_w:hp * pair_w + HEAD_DIM, rs] = (lo * kscale_ref[hp]).astype(BF16)
                    kt_scr[hp * pair_w + HEAD_DIM:(hp + 1) * pair_w, rs] = (hi * kscale_ref[hp]).astype(BF16)
        elif group == V_GROUP:
            v_scr[slot, :, lanes] = val.astype(BF16)
        elif group == GRET_GROUP:
            gret_scr[slot, :, lanes] = val
        elif group == GLRU_GROUP:
            glru_scr[slot, :, lanes] = val
        else:
            taps = CONV_WIDTH - 1
            x4 = val.reshape(GROUPS_PER_HALF, RUN, V7X_SUBLANES, MXU_PIECE_COLS)
            prev_tail = tail_scr[:, :, lanes]
            if tail_is_zero is not None:
                prev_tail = jnp.where(tail_is_zero, 0.0, prev_tail)
            tail_scr[:, :, lanes] = x4[GROUPS_PER_HALF - 1, RUN - taps:RUN]
            late = x4[:, RUN - taps:RUN]
            late_prev = jnp.concatenate([prev_tail[None], late[:-1]], axis=0)
            flat = (GROUPS_PER_HALF * taps, V7X_SUBLANES, MXU_PIECE_COLS)
            sub = lax.broadcasted_iota(jnp.int32, flat, 1)
            wrapped = jnp.where(sub >= 1, pltpu.roll(late.reshape(flat), 1, 1),
                                pltpu.roll(late_prev.reshape(flat), 1, 1)).reshape(late.shape)
            xc4 = convb_ref[:, lanes] + x4 * convw_ref[taps:CONV_WIDTH, lanes]
            for s in range(1, CONV_WIDTH):
                src = jnp.concatenate([wrapped[:, taps - s:], x4[:, :RUN - s]], axis=1)
                xc4 = xc4 + src * convw_ref[taps - s:CONV_WIDTH - s, lanes]
            xc = xc4.reshape(hb, MXU_PIECE_COLS)
            xc_scr[slot, :, lanes] = xc
            xcb_scr[slot, :, lanes] = xc.astype(BF16)

    def block_diagonal(top_left, bottom_right):
        zeros = jnp.zeros_like(top_left)
        return jnp.concatenate([jnp.concatenate([top_left, zeros], axis=1),
                                jnp.concatenate([zeros, bottom_right], axis=1)], axis=0)

    half_dim = HEAD_DIM // 2

    def pair_rows(top, bottom):
        return jnp.concatenate([block_diagonal(top[:half_dim], top[half_dim:]),
                                block_diagonal(bottom[:half_dim], bottom[half_dim:])], axis=0)

    def retention_scores(slot, hp, c):
        rs = chunk_rows(c)
        pair = slice(2 * hp * HEAD_DIM, (2 * hp + 2) * HEAD_DIM)
        keys = pair_rows(kt_scr[pair.start:pair.start + HEAD_DIM, rs],
                         kt_scr[pair.start + HEAD_DIM:pair.stop, rs])
        scores = jnp.dot(q_scr[slot, rs, pair], keys, preferred_element_type=F32)
        mask = jnp.concatenate([causal, causal], axis=1)
        sd_scr[slot, rs, pair] = jnp.where(mask, scores, 0.0).astype(BF16)

    def retention_summaries(slot, h):
        hl = head_lanes(h)
        base = (h // 2) * 2 * HEAD_DIM + (h % 2) * half_dim
        kt = jnp.concatenate([kt_scr[base:base + half_dim, :],
                              kt_scr[base + HEAD_DIM:base + HEAD_DIM + half_dim, :]], axis=0)
        values = block_diagonal(v_scr[slot, chunk_rows(0), hl], v_scr[slot, chunk_rows(1), hl])
        kv = jnp.dot(kt, values, preferred_element_type=F32)
        kv_scr[slot, 0, h] = kv[:, :HEAD_DIM]
        kv_scr[slot, 1, h] = kv[:, HEAD_DIM:]

    def projection_stage(slot, rot, tail_is_zero):
        piece = lambda p: functools.partial(in_piece, slot, p, rot, tail_is_zero)
        heads = lambda part: range(part * HEADS_PER_PIECE, (part + 1) * HEADS_PER_PIECE)
        prods = lambda part: (
            [functools.partial(retention_scores, slot, h // 2, c)
             for h in heads(part)[::2] for c in range(CHUNKS_PER_HALF)]
            + [functools.partial(retention_summaries, slot, h) for h in heads(part)])
        order = []
        for part in range(PIECES_PER_GROUP):
            order += [piece(g * PIECES_PER_GROUP + part) for g in (Q_GROUP, K_GROUP, V_GROUP)]
            order += prods(part)
        order += [piece(g * PIECES_PER_GROUP + part) for g in (GRET_GROUP, XLRU_GROUP, GLRU_GROUP)
                  for part in range(PIECES_PER_GROUP)]
        return order

    def retention_unit(slot, hp, c):
        rs = chunk_rows(c)
        h0, h1 = 2 * hp, 2 * hp + 1
        pair = slice(h0 * HEAD_DIM, (h1 + 1) * HEAD_DIM)
        s0, s1 = state_scr[h0], state_scr[h1]
        b0, b1 = s0.astype(BF16), s1.astype(BF16)
        lhs = jnp.concatenate([sd_scr[slot, rs, pair], q_scr[slot, rs, pair]], axis=1)
        rhs = jnp.concatenate([
            block_diagonal(v_scr[slot, rs, head_lanes(h0)], v_scr[slot, rs, head_lanes(h1)]),
            pair_rows(jnp.concatenate([b0[:half_dim], b1[:half_dim]], axis=0),
                      jnp.concatenate([b0[half_dim:], b1[half_dim:]], axis=0))], axis=0)
        o2 = jnp.dot(lhs, rhs, preferred_element_type=F32)
        for h, state, lanes in ((h0, s0, slice(0, HEAD_DIM)), (h1, s1, slice(HEAD_DIM, 2 * HEAD_DIM))):
            hl = head_lanes(h)
            o = o2[:, lanes] * xi_ref[:, hl]
            state_scr[h] = gdec_ref[:, hl] * (state + kv_scr[slot, c, h])
            mu = jnp.mean(o, axis=-1, keepdims=True)
            oc = o - mu
            var = jnp.mean(oc * oc, axis=-1, keepdims=True)
            gate = gret_scr[slot, rs, hl]
            y = oc * lax.rsqrt(var + NORM_EPS) * (gate * _sigmoid(gate))
            mixed_scr[slot, rs, hl] = y.astype(BF16)

    def lru_unit(slot, n, rh):
        sl = slice(n * LRU_BLOCK_DIM, (n + 1) * LRU_BLOCK_DIM)
        rows = slice(rh * LRU_ROWS, (rh + 1) * LRU_ROWS)
        xc = xc_scr[slot, rows, sl]
        pre = jnp.dot(xcb_scr[slot, rows, sl], wg_ref[n], preferred_element_type=F32)
        z = -lam_ref[:, sl]
        softplus = jnp.maximum(z, 0.0) + jnp.log1p(jnp.exp(-jnp.abs(z)))
        r = _sigmoid(pre[:, :LRU_BLOCK_DIM] + ba_ref[:, sl])
        i = _sigmoid(pre[:, LRU_BLOCK_DIM:] + bx_ref[:, sl])
        a = jnp.exp2(r * ((-LRU_C * LOG2_E) * softplus))
        v = 1.0 - a * a
        mult = jnp.where(v > 0.0, v * lax.rsqrt(v), 0.0)
        b = mult * (i * xc)
        hseq, hcarry_scr[:, sl] = _linear_scan_interleaved(a, b, hcarry_scr[:, sl])
        gate = glru_scr[slot, rows, sl]
        y = hseq * (gate * _sigmoid(gate))
        for g in range(LRU_ROWS // GROUP_ROWS):
            for i in range(RUN):
                src = slice(g * GROUP_ROWS + i * V7X_SUBLANES, g * GROUP_ROWS + (i + 1) * V7X_SUBLANES)
                dst = pl.ds(rows.start + g * GROUP_ROWS + i, V7X_SUBLANES, stride=RUN)
                yslab_scr[slot, n, dst, :] = y[src]

    def pack_lru_outputs(half):
        for n in range(LRU_BLOCKS):
            osl = slice(D_MODEL + n * LRU_BLOCK_DIM, D_MODEL + (n + 1) * LRU_BLOCK_DIM)
            mixed_scr[half, :, osl] = yslab_scr[half, n].astype(BF16)

    def mixer_stage(slot):
        lru = [functools.partial(lru_unit, slot, n, rh)
               for n in range(LRU_BLOCKS) for rh in range(HALF_BLOCK // LRU_ROWS)]
        ret = [functools.partial(retention_unit, slot, hp, c)
               for hp in range(RET_HEADS // 2) for c in range(CHUNKS_PER_HALF)]
        return _interleave(lru, ret)

    def out_piece(half, x_slot, p):
        cols = slice(p * MXU_PIECE_COLS, (p + 1) * MXU_PIECE_COLS)
        hres_scr[:, cols] = xring_scr[x_slot, :, cols] + jnp.dot(
            mixed_scr[half], wout_scr[:, cols], preferred_element_type=F32)

    def norm_output(half):
        hres = hres_scr[...]
        ms = jnp.mean(hres * hres, axis=-1, keepdims=True)
        y = hres * lax.rsqrt(ms + NORM_EPS) * gout_ref[...]
        if half == 0:
            out_stage[...] = y
        else:
            out_ref[0, hb:2 * hb, :] = y

    def output_stage(half, x_slot):
        return [functools.partial(pack_lru_outputs, half)] + [
            functools.partial(out_piece, half, x_slot, p) for p in range(PIECES_PER_GROUP)] + [
            functools.partial(norm_output, half)]

    halves_per_row = 2 * steps_per_row
    n_halves = 2 * n_steps
    first = 2 * step

    def x_copy(g, slot):
        start = (g % halves_per_row) * hb
        rows = pl.ds(start if isinstance(start, int) else pl.multiple_of(start, hb), hb)
        return pltpu.make_async_copy(x_hbm.at[g // halves_per_row, rows, :],
                                     xring_scr.at[slot], x_sems.at[slot])

    def x_slot(k):
        return lax.rem(first + (k % X_RING), X_RING)

    @pl.when(step < n_steps)
    def _():
        @pl.when(step == 0)
        def _():
            for g in range(X_RING - 1):
                x_copy(g, g).start()
            xring_scr[X_RING - 1] = jnp.zeros(xring_scr.shape[1:], F32)
            stages = [buf.at[slot] for buf in (gret_scr, xc_scr, glru_scr) for slot in range(2)]
            _load_weights_as_bf16([(win_hbm, win_scr, (Q_GROUP, K_GROUP)), (wout_hbm, wout_scr, ())],
                                  stages, weight_sems)
            x_copy(0, 0).wait()
            norm_input(xring_scr[0], 0)
            tail_scr[...] = jnp.zeros_like(tail_scr)
            rot = (cos_ref[0:hb, :], sin_ref[0:hb, :], cost_ref[:, 0:hb], sint_ref[:, 0:hb])
            for thunk in projection_stage(0, rot, None):
                thunk()
            mixed_scr[1] = jnp.zeros(mixed_scr.shape[1:], BF16)
            yslab_scr[1] = jnp.zeros(yslab_scr.shape[1:], F32)
            out_stage[...] = jnp.zeros_like(out_stage)

        @pl.when(step % steps_per_row == 0)
        def _():
            state_scr[...] = jnp.zeros_like(state_scr)
            hcarry_scr[...] = jnp.zeros_like(hcarry_scr)

        @pl.when(jnp.logical_and(step > 0, first + 2 < n_halves))
        def _():
            x_copy(first + 2, x_slot(2)).start()

        x_copy(first + 1, x_slot(1)).wait()
        out_ref[0, 0:hb, :] = out_stage[...]
        norm_input(xring_scr[x_slot(1)], 1)
        rot = (cos_ref[hb:2 * hb, :], sin_ref[hb:2 * hb, :], cost_ref[:, hb:2 * hb], sint_ref[:, hb:2 * hb])
        matrix = output_stage(1, x_slot(-1)) + projection_stage(1, rot, None)
        for thunk in _interleave(mixer_stage(0), matrix):
            thunk()

        @pl.when(first + 3 < n_halves)
        def _():
            x_copy(first + 3, x_slot(3)).start()

        @pl.when(first + 2 < n_halves)
        def _():
            x_copy(first + 2, x_slot(2)).wait()

        norm_input(xring_scr[x_slot(2)], 0)
        next_starts_row = (step + 1) % steps_per_row == 0
        rot = (cosn_ref[...], sinn_ref[...], cosnt_ref[...], sinnt_ref[...])
        matrix = output_stage(0, x_slot(0)) + projection_stage(0, rot, next_starts_row)
        for thunk in _interleave(mixer_stage(1), matrix):
            thunk()

    @pl.when(step == n_steps)
    def _():
        out_ref[0, 0:hb, :] = out_stage[...]
        for thunk in output_stage(1, (n_halves - 1) % X_RING):
            thunk()


def _position_tables(seq_len):
    half = np.arange(0, HEAD_DIM, 2, dtype=np.float64)
    inv_freq = ROPE_BASE ** (-half / HEAD_DIM)
    ang = np.arange(seq_len, dtype=np.float64)[:, None] * inv_freq[None, :]
    cos, sin = np.cos(ang), np.sin(ang)
    cos_t = np.concatenate([cos, cos], axis=-1).astype(np.float32)
    sin_t = np.concatenate([sin, sin], axis=-1).astype(np.float32)
    return cos_t, sin_t


def _decay_tables():
    c = RET_CHUNK
    scale = HEAD_DIM ** -0.5
    log_g = np.log1p(-np.exp2(-5.0 - np.arange(RET_HEADS, dtype=np.float64)))
    idx = np.arange(c, dtype=np.float64)
    xi = np.exp((idx + 1)[None, :] * log_g[:, None])
    kscale = scale * np.exp(-(idx + 1)[None, :] * log_g[:, None])
    gdec = np.exp(c * log_g)

    def rows_by_head_lanes(t):
        return np.repeat(t.T[:, :, None], HEAD_DIM, axis=2).reshape(c, RET_HEADS * HEAD_DIM)

    f32 = lambda t: t.astype(np.float32)
    kscale_pairs = np.repeat(kscale.reshape(RET_HEADS // 2, 2, 1, c), HEAD_DIM // 2, axis=2)
    kscale_pairs = kscale_pairs.reshape(RET_HEADS // 2, HEAD_DIM, c)
    return f32(kscale_pairs), f32(rows_by_head_lanes(xi)), f32(np.repeat(gdec, HEAD_DIM)[None, :])


def _resident(arr):
    nd = arr.ndim
    return pl.BlockSpec(arr.shape, lambda i: (0,) * nd, pipeline_mode=pl.Buffered(1))


_IN_HBM = pl.BlockSpec(memory_space=pl.ANY)


@jax.jit
def kernel(x, norm_in_g, w_in, conv_w, conv_b, gate_a_w, gate_a_b, gate_x_w, gate_x_b,
           lru_lambda, w_out, norm_out_g):
    batch, seq_len, d_model = x.shape
    assert d_model == D_MODEL and seq_len % STEP_BLOCK == 0 and batch * seq_len >= X_RING * HALF_BLOCK
    assert w_in.shape == (D_MODEL, IN_WIDTH) and w_out.shape == (MIX_WIDTH, D_MODEL)
    steps_per_row = seq_len // STEP_BLOCK
    n_steps = batch * steps_per_row
    halves_per_row = 2 * steps_per_row

    cos_t, sin_t = _position_tables(seq_len)
    kscale_t, xi_t, gdec_t = _decay_tables()
    w_gates = jnp.concatenate([gate_a_w, gate_x_w], axis=-1).astype(BF16)
    row = lambda p: p.reshape(1, D_MODEL).astype(F32)

    def next_half(i):
        n = jnp.minimum(2 * (i + 1), 2 * n_steps - 2)
        return (n // halves_per_row, n % halves_per_row, 0)

    def block_of_step(i):
        j = jnp.minimum(i, n_steps - 1)
        return (j // steps_per_row, j % steps_per_row, 0)

    def block_of_previous_step(i):
        j = jnp.maximum(i - 1, 0)
        return (j // steps_per_row, j % steps_per_row, 0)

    step_table = pl.BlockSpec((STEP_BLOCK, HEAD_DIM), lambda i: block_of_step(i)[1:])
    next_table = pl.BlockSpec((HALF_BLOCK, HEAD_DIM), lambda i: next_half(i)[1:])
    step_table_ft = pl.BlockSpec((HEAD_DIM, STEP_BLOCK), lambda i: (0, block_of_step(i)[1]))
    next_table_ft = pl.BlockSpec((HEAD_DIM, HALF_BLOCK), lambda i: (0, next_half(i)[1]))
    cos_ft, sin_ft = np.ascontiguousarray(cos_t.T), np.ascontiguousarray(sin_t.T)
    small = (row(norm_in_g), conv_w.astype(F32), row(conv_b), w_gates, row(gate_a_b),
             row(gate_x_b), row(lru_lambda), row(norm_out_g), kscale_t, xi_t, gdec_t)
    (gin, convw, convb, wg, ba, bx, lam, gout, kscale_t, xi_t, gdec_t) = small
    operands_and_specs = (
        (x, _IN_HBM),
        (cos_t, step_table), (sin_t, step_table), (cos_t, next_table), (sin_t, next_table),
        (cos_ft, step_table_ft), (sin_ft, step_table_ft), (cos_ft, next_table_ft), (sin_ft, next_table_ft),
        (gin, _resident(gin)), (w_in.astype(F32), _IN_HBM), (convw, _resident(convw)),
        (convb, _resident(convb)), (wg, _resident(wg)), (ba, _resident(ba)), (bx, _resident(bx)),
        (lam, _resident(lam)), (w_out.astype(F32), _IN_HBM), (gout, _resident(gout)),
        (kscale_t, _resident(kscale_t)), (xi_t, _resident(xi_t)), (gdec_t, _resident(gdec_t)),
    )
    operands = [op for op, _ in operands_and_specs]
    in_specs = [spec for _, spec in operands_and_specs]

    half_f32 = pltpu.VMEM((2, HALF_BLOCK, D_MODEL), F32)
    half_bf16 = pltpu.VMEM((2, HALF_BLOCK, D_MODEL), BF16)
    scratch_shapes = [
        pltpu.VMEM((D_MODEL, IN_WIDTH), BF16),
        pltpu.VMEM((MIX_WIDTH, D_MODEL), BF16),
        pltpu.SemaphoreType.DMA((WEIGHT_STAGES,)),
        half_bf16,
        pltpu.VMEM((D_MODEL, HALF_BLOCK), BF16),
        half_bf16,
        half_bf16,
        pltpu.VMEM((2, CHUNKS_PER_HALF, RET_HEADS, HEAD_DIM, HEAD_DIM), F32),
        half_f32,
        half_f32,
        half_bf16,
        half_f32,
        half_bf16,
        half_bf16,
        pltpu.VMEM((D_MODEL // LRU_BLOCK_DIM, HALF_BLOCK, LRU_BLOCK_DIM), F32),
        pltpu.VMEM((2, LRU_BLOCKS, HALF_BLOCK, LRU_BLOCK_DIM), F32),
        pltpu.VMEM((HALF_BLOCK, D_MODEL), F32),
        pltpu.VMEM((RET_HEADS, HEAD_DIM, HEAD_DIM), F32),
        pltpu.VMEM((1, D_MODEL), F32),
        pltpu.VMEM((CONV_WIDTH - 1, V7X_SUBLANES, D_MODEL), F32),
        pltpu.VMEM((2, HALF_BLOCK, MIX_WIDTH), BF16),
        pltpu.VMEM((X_RING, HALF_BLOCK, D_MODEL), F32),
        pltpu.SemaphoreType.DMA((X_RING,)),
        pltpu.VMEM((HALF_BLOCK, D_MODEL), F32),
    ]

    nbytes = lambda shape, dtype: int(np.prod(shape)) * jnp.dtype(dtype).itemsize
    f32_rows = lambda rows, width: nbytes((rows, width), F32)
    windows = 2 * (f32_rows(STEP_BLOCK, D_MODEL)
                   + 4 * f32_rows(STEP_BLOCK + HALF_BLOCK, HEAD_DIM))
    vmem_bytes = (sum(nbytes(s.shape, s.dtype) for s in scratch_shapes if s.memory_space == pltpu.VMEM)
                  + windows + sum(nbytes(a.shape, a.dtype) for a in small) + COMPILER_SCRATCH_BYTES)
    assert vmem_bytes <= V7X_VMEM_BYTES, vmem_bytes

    return pl.pallas_call(
        functools.partial(_layer_body, steps_per_row=steps_per_row, n_steps=n_steps),
        grid=(n_steps + 1,),
        in_specs=in_specs,
        out_specs=pl.BlockSpec((1, STEP_BLOCK, D_MODEL), block_of_previous_step),
        out_shape=jax.ShapeDtypeStruct(x.shape, x.dtype),
        scratch_shapes=scratch_shapes,
        compiler_params=pltpu.CompilerParams(
            dimension_semantics=("arbitrary",),
            vmem_limit_bytes=vmem_bytes,
        ),
        name="hybrid_layer",
    )(*operands)
```

```python
import functools

import jax
import jax.numpy as jnp
import numpy as np
from jax import lax
from jax.experimental import pallas as pl
from jax.experimental.pallas import tpu as pltpu

D_MODEL = 1024
RET_HEADS = 8
HEAD_DIM = 128
RET_CHUNK = 128
ROPE_BASE = 10000.0
LRU_BLOCKS = 8
LRU_BLOCK_DIM = 128
LRU_C = 8.0
CONV_WIDTH = 4
MIX_WIDTH = 2 * D_MODEL
IN_WIDTH = 6 * D_MODEL
NORM_EPS = 1e-6

V7X_SUBLANES = 8
HALF_BLOCK = 256
STEP_BLOCK = 2 * HALF_BLOCK
CHUNKS_PER_HALF = HALF_BLOCK // RET_CHUNK
assert CHUNKS_PER_HALF == 2 and RET_HEADS % 2 == 0
MXU_PIECE_COLS = 512
HEADS_PER_PIECE = MXU_PIECE_COLS // HEAD_DIM
PIECES_PER_GROUP = D_MODEL // MXU_PIECE_COLS
RUN = V7X_SUBLANES
GROUP_ROWS = RUN * V7X_SUBLANES
GROUPS_PER_HALF = HALF_BLOCK // GROUP_ROWS
LRU_ROWS = 2 * GROUP_ROWS
WEIGHT_BLOCK_ROWS = HALF_BLOCK
WEIGHT_STAGES = 6
V7X_VMEM_BYTES = 64 * 1024 * 1024
COMPILER_SCRATCH_BYTES = 8 * 1024 * 1024

Q_GROUP, K_GROUP, V_GROUP, GRET_GROUP, XLRU_GROUP, GLRU_GROUP = range(6)

F32 = jnp.float32
BF16 = jnp.bfloat16
LOG2_E = 1.4426950408889634


def _sigmoid(z):
    return 1.0 / (1.0 + jnp.exp2(z * (-LOG2_E)))


def _rotate_pair(lo, hi, cos, sin):
    return lo * cos - hi * sin, lo * sin + hi * cos


def _linear_scan_interleaved(a, b, h0):
    lanes = a.shape[1]
    n_groups = a.shape[0] // GROUP_ROWS
    a4 = a.reshape(n_groups, RUN, V7X_SUBLANES, lanes)
    b4 = b.reshape(n_groups, RUN, V7X_SUBLANES, lanes)
    sub = lax.broadcasted_iota(jnp.int32, (V7X_SUBLANES, lanes), 0)
    carry = jnp.broadcast_to(h0, (V7X_SUBLANES, lanes))
    out = []
    for g in range(n_groups):
        h_loc, a_cum = [b4[g, 0]], [a4[g, 0]]
        for i in range(1, RUN):
            h_loc.append(a4[g, i] * h_loc[-1] + b4[g, i])
            a_cum.append(a4[g, i] * a_cum[-1])
        p, e = a_cum[-1], h_loc[-1]
        shift = 1
        while shift < V7X_SUBLANES:
            keep = sub >= shift
            p_prev = jnp.where(keep, pltpu.roll(p, shift, 0), 1.0)
            e_prev = jnp.where(keep, pltpu.roll(e, shift, 0), 0.0)
            e = p * e_prev + e
            p = p * p_prev
            shift *= 2
        first = sub >= 1
        entry = (jnp.where(first, pltpu.roll(e, 1, 0), 0.0)
                 + jnp.where(first, pltpu.roll(p, 1, 0), 1.0) * carry)
        hs = [h_loc[i] + a_cum[i] * entry for i in range(RUN)]
        out += hs
        last = hs[-1][V7X_SUBLANES - 1:V7X_SUBLANES, :]
        carry = jnp.broadcast_to(last, (V7X_SUBLANES, lanes))
    return jnp.concatenate(out, axis=0), last


def _pair_layout(block):
    half = HEAD_DIM // 2
    parts = []
    for base in range(0, block.shape[1], 2 * HEAD_DIM):
        for start in (0, HEAD_DIM, half, HEAD_DIM + half):
            parts.append(block[:, base + start:base + start + half])
    return jnp.concatenate(parts, axis=1)


def _load_weights_as_bf16(weights, stages, sems):
    blocks = [(w_hbm, w_scr, r, c, c in paired)
              for w_hbm, w_scr, paired in weights
              for r in range(w_hbm.shape[0] // WEIGHT_BLOCK_ROWS)
              for c in range(w_hbm.shape[1] // D_MODEL)]
    n_stage = len(stages)

    def block_copy(k):
        w_hbm, _, r, c, _ = blocks[k]
        src = w_hbm.at[pl.ds(r * WEIGHT_BLOCK_ROWS, WEIGHT_BLOCK_ROWS), pl.ds(c * D_MODEL, D_MODEL)]
        return pltpu.make_async_copy(src, stages[k % n_stage], sems.at[k % n_stage])

    for k in range(min(n_stage, len(blocks))):
        block_copy(k).start()
    for k, (_, w_scr, r, c, paired) in enumerate(blocks):
        block_copy(k).wait()
        block = stages[k % n_stage][...]
        w_scr[r * WEIGHT_BLOCK_ROWS:(r + 1) * WEIGHT_BLOCK_ROWS, c * D_MODEL:(c + 1) * D_MODEL] = (
            (_pair_layout(block) if paired else block).astype(BF16))
        if k + n_stage < len(blocks):
            block_copy(k + n_stage).start()


def _interleave(vector_units, matrix_units):
    order, i, j = [], 0, 0
    nv, nm = max(len(vector_units), 1), max(len(matrix_units), 1)
    while i < len(vector_units) or j < len(matrix_units):
        if j < len(matrix_units) and (i >= len(vector_units) or j * nv <= i * nm):
            order.append(matrix_units[j])
            j += 1
        else:
            order.append(vector_units[i])
            i += 1
    return order


def _layer_body(x_ref, xnext_ref, cos_ref, sin_ref, cosn_ref, sinn_ref, cost_ref, sint_ref,
                cosnt_ref, sinnt_ref, gin_ref, win_hbm,
                convw_ref, convb_ref, wg_ref, ba_ref, bx_ref, lam_ref, wout_hbm, gout_ref,
                kscale_ref, xi_ref, gdec_ref, out_ref,
                win_scr, wout_scr, weight_sems,
                q_scr, kt_scr, v_scr, sd_scr, kv_scr, gret_scr, xc_scr, xcb_scr, glru_scr,
                xn_scr, xnp_scr, xslab_scr, yslab_scr, hres_scr, state_scr, hcarry_scr, tail_scr,
                mixed_scr, xstash_scr, out_stage, *, steps_per_row, n_steps):
    hb = HALF_BLOCK
    step = pl.program_id(0)
    causal = (lax.broadcasted_iota(jnp.int32, (RET_CHUNK, RET_CHUNK), 0)
              >= lax.broadcasted_iota(jnp.int32, (RET_CHUNK, RET_CHUNK), 1))

    def head_lanes(h):
        return slice(h * HEAD_DIM, (h + 1) * HEAD_DIM)

    def chunk_rows(c):
        return slice(c * RET_CHUNK, (c + 1) * RET_CHUNK)

    def norm_input(x, slot):
        ms = jnp.mean(x * x, axis=-1, keepdims=True)
        inv_rms = lax.rsqrt(ms + NORM_EPS)
        for nb in range(D_MODEL // LRU_BLOCK_DIM):
            sl = slice(nb * LRU_BLOCK_DIM, (nb + 1) * LRU_BLOCK_DIM)
            xn = x[:, sl] * inv_rms * gin_ref[:, sl]
            xn_scr[slot, :, sl] = xn.astype(BF16)
            xslab_scr[nb] = xn
        pieces = []
        for g in range(GROUPS_PER_HALF):
            for i in range(RUN):
                rows = pl.ds(g * GROUP_ROWS + i, V7X_SUBLANES, stride=RUN)
                pieces.append(jnp.concatenate(
                    [xslab_scr[nb, rows, :] for nb in range(D_MODEL // LRU_BLOCK_DIM)], axis=1))
        xnp_scr[slot] = jnp.concatenate(pieces, axis=0).astype(BF16)

    def in_piece(slot, p, rot, tail_is_zero):
        group, part = divmod(p, PIECES_PER_GROUP)
        cols = slice(p * MXU_PIECE_COLS, (p + 1) * MXU_PIECE_COLS)
        lanes = slice(part * MXU_PIECE_COLS, (part + 1) * MXU_PIECE_COLS)
        lhs = xnp_scr[slot] if group in (XLRU_GROUP, GLRU_GROUP) else xn_scr[slot]
        val = jnp.dot(lhs, win_scr[:, cols], preferred_element_type=F32)
        cos_t, sin_t, cos_ft, sin_ft = rot
        pair_w = 2 * HEAD_DIM
        if group == Q_GROUP:
            for j in range(MXU_PIECE_COLS // pair_w):
                lo, hi = _rotate_pair(val[:, j * pair_w:j * pair_w + HEAD_DIM],
                                      val[:, j * pair_w + HEAD_DIM:(j + 1) * pair_w], cos_t, sin_t)
                gp = part * MXU_PIECE_COLS + j * pair_w
                q_scr[slot, :, gp:gp + pair_w] = jnp.concatenate([lo, hi], axis=1).astype(BF16)
        elif group == K_GROUP:
            for j in range(MXU_PIECE_COLS // pair_w):
                hp = (part * MXU_PIECE_COLS + j * pair_w) // pair_w
                for c in range(CHUNKS_PER_HALF):
                    rs = chunk_rows(c)
                    lo, hi = _rotate_pair(val[rs, j * pair_w:j * pair_w + HEAD_DIM].T,
                                          val[rs, j * pair_w + HEAD_DIM:(j + 1) * pair_w].T,
                                          cos_ft[:, rs], sin_ft[:, rs])
                    kt_scr[hp * pair_w:hp * pair_w + HEAD_DIM, rs] = (lo * kscale_ref[hp]).astype(BF16)
                    kt_scr[hp * pair_w + HEAD_DIM:(hp + 1) * pair_w, rs] = (hi * kscale_ref[hp]).astype(BF16)
        elif group == V_GROUP:
            v_scr[slot, :, lanes] = val.astype(BF16)
        elif group == GRET_GROUP:
            gret_scr[slot, :, lanes] = val
        elif group == GLRU_GROUP:
            glru_scr[slot, :, lanes] = val
        else:
            taps = CONV_WIDTH - 1
            x4 = val.reshape(GROUPS_PER_HALF, RUN, V7X_SUBLANES, MXU_PIECE_COLS)
            prev_tail = tail_scr[:, :, lanes]
            if tail_is_zero is not None:
                prev_tail = jnp.where(tail_is_zero, 0.0, prev_tail)
            tail_scr[:, :, lanes] = x4[GROUPS_PER_HALF - 1, RUN - taps:RUN]
            late = x4[:, RUN - taps:RUN]
            late_prev = jnp.concatenate([prev_tail[None], late[:-1]], axis=0)
            flat = (GROUPS_PER_HALF * taps, V7X_SUBLANES, MXU_PIECE_COLS)
            sub = lax.broadcasted_iota(jnp.int32, flat, 1)
            wrapped = jnp.where(sub >= 1, pltpu.roll(late.reshape(flat), 1, 1),
                                pltpu.roll(late_prev.reshape(flat), 1, 1)).reshape(late.shape)
            xc4 = convb_ref[:, lanes] + x4 * convw_ref[taps:CONV_WIDTH, lanes]
            for s in range(1, CONV_WIDTH):
                src = jnp.concatenate([wrapped[:, taps - s:], x4[:, :RUN - s]], axis=1)
                xc4 = xc4 + src * convw_ref[taps - s:CONV_WIDTH - s, lanes]
            xc = xc4.reshape(hb, MXU_PIECE_COLS)
            xc_scr[slot, :, lanes] = xc
            xcb_scr[slot, :, lanes] = xc.astype(BF16)

    def block_diagonal(top_left, bottom_right):
        zeros = jnp.zeros_like(top_left)
        return jnp.concatenate([jnp.concatenate([top_left, zeros], axis=1),
                                jnp.concatenate([zeros, bottom_right], axis=1)], axis=0)

    half_dim = HEAD_DIM // 2

    def pair_rows(top, bottom):
        return jnp.concatenate([block_diagonal(top[:half_dim], top[half_dim:]),
                                block_diagonal(bottom[:half_dim], bottom[half_dim:])], axis=0)

    def retention_scores(slot, hp, c):
        rs = chunk_rows(c)
        pair = slice(2 * hp * HEAD_DIM, (2 * hp + 2) * HEAD_DIM)
        keys = pair_rows(kt_scr[pair.start:pair.start + HEAD_DIM, rs],
                         kt_scr[pair.start + HEAD_DIM:pair.stop, rs])
        scores = jnp.dot(q_scr[slot, rs, pair], keys, preferred_element_type=F32)
        mask = jnp.concatenate([causal, causal], axis=1)
        sd_scr[slot, rs, pair] = jnp.where(mask, scores, 0.0).astype(BF16)

    def retention_summaries(slot, h):
        hl = head_lanes(h)
        base = (h // 2) * 2 * HEAD_DIM + (h % 2) * half_dim
        kt = jnp.concatenate([kt_scr[base:base + half_dim, :],
                              kt_scr[base + HEAD_DIM:base + HEAD_DIM + half_dim, :]], axis=0)
        values = block_diagonal(v_scr[slot, chunk_rows(0), hl], v_scr[slot, chunk_rows(1), hl])
        kv = jnp.dot(kt, values, preferred_element_type=F32)
        kv_scr[slot, 0, h] = kv[:, :HEAD_DIM]
        kv_scr[slot, 1, h] = kv[:, HEAD_DIM:]

    def projection_stage(slot, rot, tail_is_zero):
        piece = lambda p: functools.partial(in_piece, slot, p, rot, tail_is_zero)
        heads = lambda part: range(part * HEADS_PER_PIECE, (part + 1) * HEADS_PER_PIECE)
        prods = lambda part: (
            [functools.partial(retention_scores, slot, h // 2, c)
             for h in heads(part)[::2] for c in range(CHUNKS_PER_HALF)]
            + [functools.partial(retention_summaries, slot, h) for h in heads(part)])
        order = []
        for part in range(PIECES_PER_GROUP):
            order += [piece(g * PIECES_PER_GROUP + part) for g in (Q_GROUP, K_GROUP, V_GROUP)]
            order += prods(part)
        order += [piece(g * PIECES_PER_GROUP + part) for g in (GRET_GROUP, XLRU_GROUP, GLRU_GROUP)
                  for part in range(PIECES_PER_GROUP)]
        return order

    def retention_unit(slot, hp, c):
        rs = chunk_rows(c)
        h0, h1 = 2 * hp, 2 * hp + 1
        pair = slice(h0 * HEAD_DIM, (h1 + 1) * HEAD_DIM)
        s0, s1 = state_scr[h0], state_scr[h1]
        b0, b1 = s0.astype(BF16), s1.astype(BF16)
        lhs = jnp.concatenate([sd_scr[slot, rs, pair], q_scr[slot, rs, pair]], axis=1)
        rhs = jnp.concatenate([
            block_diagonal(v_scr[slot, rs, head_lanes(h0)], v_scr[slot, rs, head_lanes(h1)]),
            pair_rows(jnp.concatenate([b0[:half_dim], b1[:half_dim]], axis=0),
                      jnp.concatenate([b0[half_dim:], b1[half_dim:]], axis=0))], axis=0)
        o2 = jnp.dot(lhs, rhs, preferred_element_type=F32)
        for h, state, lanes in ((h0, s0, slice(0, HEAD_DIM)), (h1, s1, slice(HEAD_DIM, 2 * HEAD_DIM))):
            hl = head_lanes(h)
            o = o2[:, lanes] * xi_ref[:, hl]
            state_scr[h] = gdec_ref[:, hl] * (state + kv_scr[slot, c, h])
            mu = jnp.mean(o, axis=-1, keepdims=True)
            oc = o - mu
            var = jnp.mean(oc * oc, axis=-1, keepdims=True)
            gate = gret_scr[slot, rs, hl]
            y = oc * lax.rsqrt(var + NORM_EPS) * (gate * _sigmoid(gate))
            mixed_scr[slot, rs, hl] = y.astype(BF16)

    def lru_unit(slot, n, rh):
        sl = slice(n * LRU_BLOCK_DIM, (n + 1) * LRU_BLOCK_DIM)
        rows = slice(rh * LRU_ROWS, (rh + 1) * LRU_ROWS)
        xc = xc_scr[slot, rows, sl]
        pre = jnp.dot(xcb_scr[slot, rows, sl], wg_ref[n], preferred_element_type=F32)
        z = -lam_ref[:, sl]
        softplus = jnp.maximum(z, 0.0) + jnp.log1p(jnp.exp(-jnp.abs(z)))
        r = _sigmoid(pre[:, :LRU_BLOCK_DIM] + ba_ref[:, sl])
        i = _sigmoid(pre[:, LRU_BLOCK_DIM:] + bx_ref[:, sl])
        a = jnp.exp2(r * ((-LRU_C * LOG2_E) * softplus))
        v = 1.0 - a * a
        mult = jnp.where(v > 0.0, v * lax.rsqrt(v), 0.0)
        b = mult * (i * xc)
        hseq, hcarry_scr[:, sl] = _linear_scan_interleaved(a, b, hcarry_scr[:, sl])
        gate = glru_scr[slot, rows, sl]
        y = hseq * (gate * _sigmoid(gate))
        for g in range(LRU_ROWS // GROUP_ROWS):
            for i in range(RUN):
                src = slice(g * GROUP_ROWS + i * V7X_SUBLANES, g * GROUP_ROWS + (i + 1) * V7X_SUBLANES)
                dst = pl.ds(rows.start + g * GROUP_ROWS + i, V7X_SUBLANES, stride=RUN)
                yslab_scr[slot, n, dst, :] = y[src]

    def pack_lru_outputs(half):
        for n in range(LRU_BLOCKS):
            osl = slice(D_MODEL + n * LRU_BLOCK_DIM, D_MODEL + (n + 1) * LRU_BLOCK_DIM)
            mixed_scr[half, :, osl] = yslab_scr[half, n].astype(BF16)

    def mixer_stage(slot):
        lru = [functools.partial(lru_unit, slot, n, rh)
               for n in range(LRU_BLOCKS) for rh in range(HALF_BLOCK // LRU_ROWS)]
        ret = [functools.partial(retention_unit, slot, hp, c)
               for hp in range(RET_HEADS // 2) for c in range(CHUNKS_PER_HALF)]
        return _interleave(lru, ret)

    def out_piece(half, p):
        cols = slice(p * MXU_PIECE_COLS, (p + 1) * MXU_PIECE_COLS)
        resid = x_ref[0, 0:hb, cols] if half == 0 else xstash_scr[:, cols]
        hres_scr[:, cols] = resid + jnp.dot(mixed_scr[half], wout_scr[:, cols],
                                            preferred_element_type=F32)

    def norm_output(half):
        hres = hres_scr[...]
        ms = jnp.mean(hres * hres, axis=-1, keepdims=True)
        y = hres * lax.rsqrt(ms + NORM_EPS) * gout_ref[...]
        if half == 0:
            out_stage[...] = y
        else:
            out_ref[0, hb:2 * hb, :] = y

    def output_stage(half):
        return [functools.partial(pack_lru_outputs, half)] + [
            functools.partial(out_piece, half, p) for p in range(PIECES_PER_GROUP)] + [
            functools.partial(norm_output, half)]

    @pl.when(step < n_steps)
    def _():
        @pl.when(step == 0)
        def _():
            stages = [buf.at[slot] for buf in (gret_scr, xc_scr, glru_scr) for slot in range(2)]
            _load_weights_as_bf16([(win_hbm, win_scr, (Q_GROUP, K_GROUP)), (wout_hbm, wout_scr, ())],
                                  stages, weight_sems)
            norm_input(x_ref[0, 0:hb, :], 0)
            tail_scr[...] = jnp.zeros_like(tail_scr)
            rot = (cos_ref[0:hb, :], sin_ref[0:hb, :], cost_ref[:, 0:hb], sint_ref[:, 0:hb])
            for thunk in projection_stage(0, rot, None):
                thunk()
            mixed_scr[1] = jnp.zeros(mixed_scr.shape[1:], BF16)
            yslab_scr[1] = jnp.zeros(yslab_scr.shape[1:], F32)
            xstash_scr[...] = jnp.zeros_like(xstash_scr)
            out_stage[...] = jnp.zeros_like(out_stage)

        @pl.when(step % steps_per_row == 0)
        def _():
            state_scr[...] = jnp.zeros_like(state_scr)
            hcarry_scr[...] = jnp.zeros_like(hcarry_scr)

        out_ref[0, 0:hb, :] = out_stage[...]
        norm_input(x_ref[0, hb:2 * hb, :], 1)
        rot = (cos_ref[hb:2 * hb, :], sin_ref[hb:2 * hb, :], cost_ref[:, hb:2 * hb], sint_ref[:, hb:2 * hb])
        matrix = output_stage(1) + projection_stage(1, rot, None)
        for thunk in _interleave(mixer_stage(0), matrix):
            thunk()
        xstash_scr[...] = x_ref[0, hb:2 * hb, :]
        norm_input(xnext_ref[0], 0)
        next_starts_row = (step + 1) % steps_per_row == 0
        rot = (cosn_ref[...], sinn_ref[...], cosnt_ref[...], sinnt_ref[...])
        matrix = output_stage(0) + projection_stage(0, rot, next_starts_row)
        for thunk in _interleave(mixer_stage(1), matrix):
            thunk()

    @pl.when(step == n_steps)
    def _():
        out_ref[0, 0:hb, :] = out_stage[...]
        for thunk in output_stage(1):
            thunk()


def _position_tables(seq_len):
    half = np.arange(0, HEAD_DIM, 2, dtype=np.float64)
    inv_freq = ROPE_BASE ** (-half / HEAD_DIM)
    ang = np.arange(seq_len, dtype=np.float64)[:, None] * inv_freq[None, :]
    cos, sin = np.cos(ang), np.sin(ang)
    cos_t = np.concatenate([cos, cos], axis=-1).astype(np.float32)
    sin_t = np.concatenate([sin, sin], axis=-1).astype(np.float32)
    return cos_t, sin_t


def _decay_tables():
    c = RET_CHUNK
    scale = HEAD_DIM ** -0.5
    log_g = np.log1p(-np.exp2(-5.0 - np.arange(RET_HEADS, dtype=np.float64)))
    idx = np.arange(c, dtype=np.float64)
    xi = np.exp((idx + 1)[None, :] * log_g[:, None])
    kscale = scale * np.exp(-(idx + 1)[None, :] * log_g[:, None])
    gdec = np.exp(c * log_g)

    def rows_by_head_lanes(t):
        return np.repeat(t.T[:, :, None], HEAD_DIM, axis=2).reshape(c, RET_HEADS * HEAD_DIM)

    f32 = lambda t: t.astype(np.float32)
    kscale_pairs = np.repeat(kscale.reshape(RET_HEADS // 2, 2, 1, c), HEAD_DIM // 2, axis=2)
    kscale_pairs = kscale_pairs.reshape(RET_HEADS // 2, HEAD_DIM, c)
    return f32(kscale_pairs), f32(rows_by_head_lanes(xi)), f32(np.repeat(gdec, HEAD_DIM)[None, :])


def _resident(arr):
    nd = arr.ndim
    return pl.BlockSpec(arr.shape, lambda i: (0,) * nd, pipeline_mode=pl.Buffered(1))


_IN_HBM = pl.BlockSpec(memory_space=pl.ANY)


@jax.jit
def kernel(x, norm_in_g, w_in, conv_w, conv_b, gate_a_w, gate_a_b, gate_x_w, gate_x_b,
           lru_lambda, w_out, norm_out_g):
    batch, seq_len, d_model = x.shape
    assert d_model == D_MODEL and seq_len % STEP_BLOCK == 0
    assert w_in.shape == (D_MODEL, IN_WIDTH) and w_out.shape == (MIX_WIDTH, D_MODEL)
    steps_per_row = seq_len // STEP_BLOCK
    n_steps = batch * steps_per_row
    halves_per_row = 2 * steps_per_row

    cos_t, sin_t = _position_tables(seq_len)
    kscale_t, xi_t, gdec_t = _decay_tables()
    w_gates = jnp.concatenate([gate_a_w, gate_x_w], axis=-1).astype(BF16)
    row = lambda p: p.reshape(1, D_MODEL).astype(F32)

    def next_half(i):
        n = jnp.minimum(2 * (i + 1), 2 * n_steps - 2)
        return (n // halves_per_row, n % halves_per_row, 0)

    def block_of_step(i):
        j = jnp.minimum(i, n_steps - 1)
        return (j // steps_per_row, j % steps_per_row, 0)

    def block_of_previous_step(i):
        j = jnp.maximum(i - 1, 0)
        return (j // steps_per_row, j % steps_per_row, 0)

    step_rows = pl.BlockSpec((1, STEP_BLOCK, D_MODEL), block_of_step)
    step_table = pl.BlockSpec((STEP_BLOCK, HEAD_DIM), lambda i: block_of_step(i)[1:])
    next_table = pl.BlockSpec((HALF_BLOCK, HEAD_DIM), lambda i: next_half(i)[1:])
    step_table_ft = pl.BlockSpec((HEAD_DIM, STEP_BLOCK), lambda i: (0, block_of_step(i)[1]))
    next_table_ft = pl.BlockSpec((HEAD_DIM, HALF_BLOCK), lambda i: (0, next_half(i)[1]))
    cos_ft, sin_ft = np.ascontiguousarray(cos_t.T), np.ascontiguousarray(sin_t.T)
    small = (row(norm_in_g), conv_w.astype(F32), row(conv_b), w_gates, row(gate_a_b),
             row(gate_x_b), row(lru_lambda), row(norm_out_g), kscale_t, xi_t, gdec_t)
    (gin, convw, convb, wg, ba, bx, lam, gout, kscale_t, xi_t, gdec_t) = small
    operands_and_specs = (
        (x, step_rows), (x, pl.BlockSpec((1, HALF_BLOCK, D_MODEL), next_half)),
        (cos_t, step_table), (sin_t, step_table), (cos_t, next_table), (sin_t, next_table),
        (cos_ft, step_table_ft), (sin_ft, step_table_ft), (cos_ft, next_table_ft), (sin_ft, next_table_ft),
        (gin, _resident(gin)), (w_in.astype(F32), _IN_HBM), (convw, _resident(convw)),
        (convb, _resident(convb)), (wg, _resident(wg)), (ba, _resident(ba)), (bx, _resident(bx)),
        (lam, _resident(lam)), (w_out.astype(F32), _IN_HBM), (gout, _resident(gout)),
        (kscale_t, _resident(kscale_t)), (xi_t, _resident(xi_t)), (gdec_t, _resident(gdec_t)),
    )
    operands = [op for op, _ in operands_and_specs]
    in_specs = [spec for _, spec in operands_and_specs]

    half_f32 = pltpu.VMEM((2, HALF_BLOCK, D_MODEL), F32)
    half_bf16 = pltpu.VMEM((2, HALF_BLOCK, D_MODEL), BF16)
    scratch_shapes = [
        pltpu.VMEM((D_MODEL, IN_WIDTH), BF16),
        pltpu.VMEM((MIX_WIDTH, D_MODEL), BF16),
        pltpu.SemaphoreType.DMA((WEIGHT_STAGES,)),
        half_bf16,
        pltpu.VMEM((D_MODEL, HALF_BLOCK), BF16),
        half_bf16,
        half_bf16,
        pltpu.VMEM((2, CHUNKS_PER_HALF, RET_HEADS, HEAD_DIM, HEAD_DIM), F32),
        half_f32,
        half_f32,
        half_bf16,
        half_f32,
        half_bf16,
        half_bf16,
        pltpu.VMEM((D_MODEL // LRU_BLOCK_DIM, HALF_BLOCK, LRU_BLOCK_DIM), F32),
        pltpu.VMEM((2, LRU_BLOCKS, HALF_BLOCK, LRU_BLOCK_DIM), F32),
        pltpu.VMEM((HALF_BLOCK, D_MODEL), F32),
        pltpu.VMEM((RET_HEADS, HEAD_DIM, HEAD_DIM), F32),
        pltpu.VMEM((1, D_MODEL), F32),
        pltpu.VMEM((CONV_WIDTH - 1, V7X_SUBLANES, D_MODEL), F32),
        pltpu.VMEM((2, HALF_BLOCK, MIX_WIDTH), BF16),
        pltpu.VMEM((HALF_BLOCK, D_MODEL), F32),
        pltpu.VMEM((HALF_BLOCK, D_MODEL), F32),
    ]

    nbytes = lambda shape, dtype: int(np.prod(shape)) * jnp.dtype(dtype).itemsize
    f32_rows = lambda rows, width: nbytes((rows, width), F32)
    windows = 2 * (f32_rows(2 * STEP_BLOCK + HALF_BLOCK, D_MODEL)
                   + 4 * f32_rows(STEP_BLOCK + HALF_BLOCK, HEAD_DIM))
    vmem_bytes = (sum(nbytes(s.shape, s.dtype) for s in scratch_shapes if s.memory_space == pltpu.VMEM)
                  + windows + sum(nbytes(a.shape, a.dtype) for a in small) + COMPILER_SCRATCH_BYTES)
    assert vmem_bytes <= V7X_VMEM_BYTES, vmem_bytes

    return pl.pallas_call(
        functools.partial(_layer_body, steps_per_row=steps_per_row, n_steps=n_steps),
        grid=(n_steps + 1,),
        in_specs=in_specs,
        out_specs=pl.BlockSpec((1, STEP_BLOCK, D_MODEL), block_of_previous_step),
        out_shape=jax.ShapeDtypeStruct(x.shape, x.dtype),
        scratch_shapes=scratch_shapes,
        compiler_params=pltpu.CompilerParams(
            dimension_semantics=("arbitrary",),
            vmem_limit_bytes=vmem_bytes,
        ),
        name="hybrid_layer",
    )(*operands)
```

```python
import functools

import jax
import jax.numpy as jnp
import numpy as np
from jax import lax
from jax.experimental import pallas as pl
from jax.experimental.pallas import tpu as pltpu

D_MODEL = 1024
RET_HEADS = 8
HEAD_DIM = 128
RET_CHUNK = 128
ROPE_BASE = 10000.0
LRU_BLOCKS = 8
LRU_BLOCK_DIM = 128
LRU_C = 8.0
CONV_WIDTH = 4
MIX_WIDTH = 2 * D_MODEL
IN_WIDTH = 6 * D_MODEL
NORM_EPS = 1e-6

V7X_SUBLANES = 8
HALF_BLOCK = 256
STEP_BLOCK = 2 * HALF_BLOCK
CHUNKS_PER_HALF = HALF_BLOCK // RET_CHUNK
assert CHUNKS_PER_HALF == 2 and RET_HEADS % 2 == 0
MXU_PIECE_COLS = 512
HEADS_PER_PIECE = MXU_PIECE_COLS // HEAD_DIM
PIECES_PER_GROUP = D_MODEL // MXU_PIECE_COLS
RUN = V7X_SUBLANES
GROUP_ROWS = RUN * V7X_SUBLANES
GROUPS_PER_HALF = HALF_BLOCK // GROUP_ROWS
LRU_ROWS = 2 * GROUP_ROWS
WEIGHT_BLOCK_ROWS = HALF_BLOCK
WEIGHT_STAGES = 6
V7X_VMEM_BYTES = 64 * 1024 * 1024
COMPILER_SCRATCH_BYTES = 8 * 1024 * 1024

Q_GROUP, K_GROUP, V_GROUP, GRET_GROUP, XLRU_GROUP, GLRU_GROUP = range(6)

F32 = jnp.float32
BF16 = jnp.bfloat16
LOG2_E = 1.4426950408889634


def _sigmoid(z):
    return 1.0 / (1.0 + jnp.exp2(z * (-LOG2_E)))


def _rotate_pair(lo, hi, cos, sin):
    return lo * cos - hi * sin, lo * sin + hi * cos


def _linear_scan_interleaved(a, b, h0):
    lanes = a.shape[1]
    n_groups = a.shape[0] // GROUP_ROWS
    a4 = a.reshape(n_groups, RUN, V7X_SUBLANES, lanes)
    b4 = b.reshape(n_groups, RUN, V7X_SUBLANES, lanes)
    sub = lax.broadcasted_iota(jnp.int32, (V7X_SUBLANES, lanes), 0)
    carry = jnp.broadcast_to(h0, (V7X_SUBLANES, lanes))
    out = []
    for g in range(n_groups):
        h_loc, a_cum = [b4[g, 0]], [a4[g, 0]]
        for i in range(1, RUN):
            h_loc.append(a4[g, i] * h_loc[-1] + b4[g, i])
            a_cum.append(a4[g, i] * a_cum[-1])
        p, e = a_cum[-1], h_loc[-1]
        shift = 1
        while shift < V7X_SUBLANES:
            keep = sub >= shift
            p_prev = jnp.where(keep, pltpu.roll(p, shift, 0), 1.0)
            e_prev = jnp.where(keep, pltpu.roll(e, shift, 0), 0.0)
            e = p * e_prev + e
            p = p * p_prev
            shift *= 2
        first = sub >= 1
        entry = (jnp.where(first, pltpu.roll(e, 1, 0), 0.0)
                 + jnp.where(first, pltpu.roll(p, 1, 0), 1.0) * carry)
        hs = [h_loc[i] + a_cum[i] * entry for i in range(RUN)]
        out += hs
        last = hs[-1][V7X_SUBLANES - 1:V7X_SUBLANES, :]
        carry = jnp.broadcast_to(last, (V7X_SUBLANES, lanes))
    return jnp.concatenate(out, axis=0), last


def _pair_layout(block):
    half = HEAD_DIM // 2
    parts = []
    for base in range(0, block.shape[1], 2 * HEAD_DIM):
        for start in (0, HEAD_DIM, half, HEAD_DIM + half):
            parts.append(block[:, base + start:base + start + half])
    return jnp.concatenate(parts, axis=1)


def _load_weights_as_bf16(weights, stages, sems):
    blocks = [(w_hbm, w_scr, r, c, c in paired)
              for w_hbm, w_scr, paired in weights
              for r in range(w_hbm.shape[0] // WEIGHT_BLOCK_ROWS)
              for c in range(w_hbm.shape[1] // D_MODEL)]
    n_stage = len(stages)

    def block_copy(k):
        w_hbm, _, r, c, _ = blocks[k]
        src = w_hbm.at[pl.ds(r * WEIGHT_BLOCK_ROWS, WEIGHT_BLOCK_ROWS), pl.ds(c * D_MODEL, D_MODEL)]
        return pltpu.make_async_copy(src, stages[k % n_stage], sems.at[k % n_stage])

    for k in range(min(n_stage, len(blocks))):
        block_copy(k).start()
    for k, (_, w_scr, r, c, paired) in enumerate(blocks):
        block_copy(k).wait()
        block = stages[k % n_stage][...]
        w_scr[r * WEIGHT_BLOCK_ROWS:(r + 1) * WEIGHT_BLOCK_ROWS, c * D_MODEL:(c + 1) * D_MODEL] = (
            (_pair_layout(block) if paired else block).astype(BF16))
        if k + n_stage < len(blocks):
            block_copy(k + n_stage).start()


def _interleave(vector_units, matrix_units):
    order, i, j = [], 0, 0
    nv, nm = max(len(vector_units), 1), max(len(matrix_units), 1)
    while i < len(vector_units) or j < len(matrix_units):
        if j < len(matrix_units) and (i >= len(vector_units) or j * nv <= i * nm):
            order.append(matrix_units[j])
            j += 1
        else:
            order.append(vector_units[i])
            i += 1
    return order


def _layer_body(x_ref, xnext_ref, cos_ref, sin_ref, cosn_ref, sinn_ref, cost_ref, sint_ref,
                cosnt_ref, sinnt_ref, gin_ref, win_hbm,
                convw_ref, convb_ref, wg_ref, ba_ref, bx_ref, lam_ref, wout_hbm, gout_ref,
                kscale_ref, xi_ref, gdec_ref, out_ref,
                win_scr, wout_scr, weight_sems,
                q_scr, kt_scr, v_scr, sd_scr, kv_scr, gret_scr, xc_scr, xcb_scr, glru_scr,
                xn_scr, xnp_scr, xslab_scr, yslab_scr, hres_scr, state_scr, hcarry_scr, tail_scr,
                mixed_scr, xstash_scr, out_stage, *, steps_per_row, n_steps):
    hb = HALF_BLOCK
    step = pl.program_id(0)
    causal = (lax.broadcasted_iota(jnp.int32, (RET_CHUNK, RET_CHUNK), 0)
              >= lax.broadcasted_iota(jnp.int32, (RET_CHUNK, RET_CHUNK), 1))

    def head_lanes(h):
        return slice(h * HEAD_DIM, (h + 1) * HEAD_DIM)

    def chunk_rows(c):
        return slice(c * RET_CHUNK, (c + 1) * RET_CHUNK)

    def norm_input(x, slot):
        ms = jnp.mean(x * x, axis=-1, keepdims=True)
        inv_rms = lax.rsqrt(ms + NORM_EPS)
        for nb in range(D_MODEL // LRU_BLOCK_DIM):
            sl = slice(nb * LRU_BLOCK_DIM, (nb + 1) * LRU_BLOCK_DIM)
            xn = x[:, sl] * inv_rms * gin_ref[:, sl]
            xn_scr[slot, :, sl] = xn.astype(BF16)
            xslab_scr[nb] = xn
        pieces = []
        for g in range(GROUPS_PER_HALF):
            for i in range(RUN):
                rows = pl.ds(g * GROUP_ROWS + i, V7X_SUBLANES, stride=RUN)
                pieces.append(jnp.concatenate(
                    [xslab_scr[nb, rows, :] for nb in range(D_MODEL // LRU_BLOCK_DIM)], axis=1))
        xnp_scr[slot] = jnp.concatenate(pieces, axis=0).astype(BF16)

    def in_piece(slot, p, rot, tail_is_zero):
        group, part = divmod(p, PIECES_PER_GROUP)
        cols = slice(p * MXU_PIECE_COLS, (p + 1) * MXU_PIECE_COLS)
        lanes = slice(part * MXU_PIECE_COLS, (part + 1) * MXU_PIECE_COLS)
        lhs = xnp_scr[slot] if group in (XLRU_GROUP, GLRU_GROUP) else xn_scr[slot]
        val = jnp.dot(lhs, win_scr[:, cols], preferred_element_type=F32)
        cos_t, sin_t, cos_ft, sin_ft = rot
        pair_w = 2 * HEAD_DIM
        if group == Q_GROUP:
            for j in range(MXU_PIECE_COLS // pair_w):
                lo, hi = _rotate_pair(val[:, j * pair_w:j * pair_w + HEAD_DIM],
                                      val[:, j * pair_w + HEAD_DIM:(j + 1) * pair_w], cos_t, sin_t)
                gp = part * MXU_PIECE_COLS + j * pair_w
                q_scr[slot, :, gp:gp + pair_w] = jnp.concatenate([lo, hi], axis=1).astype(BF16)
        elif group == K_GROUP:
            for j in range(MXU_PIECE_COLS // pair_w):
                hp = (part * MXU_PIECE_COLS + j * pair_w) // pair_w
                for c in range(CHUNKS_PER_HALF):
                    rs = chunk_rows(c)
                    lo, hi = _rotate_pair(val[rs, j * pair_w:j * pair_w + HEAD_DIM].T,
                                          val[rs, j * pair_w + HEAD_DIM:(j + 1) * pair_w].T,
                                          cos_ft[:, rs], sin_ft[:, rs])
                    kt_scr[hp * pair_w:hp * pair_w + HEAD_DIM, rs] = (lo * kscale_ref[hp]).astype(BF16)
                    kt_scr[hp * pair_w + HEAD_DIM:(hp + 1) * pair_w, rs] = (hi * kscale_ref[hp]).astype(BF16)
        elif group == V_GROUP:
            v_scr[slot, :, lanes] = val.astype(BF16)
        elif group == GRET_GROUP:
            gret_scr[slot, :, lanes] = val
        elif group == GLRU_GROUP:
            glru_scr[slot, :, lanes] = val
        else:
            taps = CONV_WIDTH - 1
            x4 = val.reshape(GROUPS_PER_HALF, RUN, V7X_SUBLANES, MXU_PIECE_COLS)
            prev_tail = tail_scr[:, :, lanes]
            if tail_is_zero is not None:
                prev_tail = jnp.where(tail_is_zero, 0.0, prev_tail)
            tail_scr[:, :, lanes] = x4[GROUPS_PER_HALF - 1, RUN - taps:RUN]
            late = x4[:, RUN - taps:RUN]
            late_prev = jnp.concatenate([prev_tail[None], late[:-1]], axis=0)
            flat = (GROUPS_PER_HALF * taps, V7X_SUBLANES, MXU_PIECE_COLS)
            sub = lax.broadcasted_iota(jnp.int32, flat, 1)
            wrapped = jnp.where(sub >= 1, pltpu.roll(late.reshape(flat), 1, 1),
                                pltpu.roll(late_prev.reshape(flat), 1, 1)).reshape(late.shape)
            xc4 = convb_ref[:, lanes] + x4 * convw_ref[taps:CONV_WIDTH, lanes]
            for s in range(1, CONV_WIDTH):
                src = jnp.concatenate([wrapped[:, taps - s:], x4[:, :RUN - s]], axis=1)
                xc4 = xc4 + src * convw_ref[taps - s:CONV_WIDTH - s, lanes]
            xc = xc4.reshape(hb, MXU_PIECE_COLS)
            xc_scr[slot, :, lanes] = xc
            xcb_scr[slot, :, lanes] = xc.astype(BF16)

    def block_diagonal(top_left, bottom_right):
        zeros = jnp.zeros_like(top_left)
        return jnp.concatenate([jnp.concatenate([top_left, zeros], axis=1),
                                jnp.concatenate([zeros, bottom_right], axis=1)], axis=0)

    half_dim = HEAD_DIM // 2

    def pair_rows(top, bottom):
        return jnp.concatenate([block_diagonal(top[:half_dim], top[half_dim:]),
                                block_diagonal(bottom[:half_dim], bottom[half_dim:])], axis=0)

    def retention_scores(slot, hp, c):
        rs = chunk_rows(c)
        pair = slice(2 * hp * HEAD_DIM, (2 * hp + 2) * HEAD_DIM)
        keys = pair_rows(kt_scr[pair.start:pair.start + HEAD_DIM, rs],
                         kt_scr[pair.start + HEAD_DIM:pair.stop, rs])
        scores = jnp.dot(q_scr[slot, rs, pair], keys, preferred_element_type=F32)
        mask = jnp.concatenate([causal, causal], axis=1)
        sd_scr[slot, rs, pair] = jnp.where(mask, scores, 0.0).astype(BF16)

    def retention_summaries(slot, h):
        hl = head_lanes(h)
        base = (h // 2) * 2 * HEAD_DIM + (h % 2) * half_dim
        kt = jnp.concatenate([kt_scr[base:base + half_dim, :],
                              kt_scr[base + HEAD_DIM:base + HEAD_DIM + half_dim, :]], axis=0)
        values = block_diagonal(v_scr[slot, chunk_rows(0), hl], v_scr[slot, chunk_rows(1), hl])
        kv = jnp.dot(kt, values, preferred_element_type=F32)
        kv_scr[slot, 0, h] = kv[:, :HEAD_DIM]
        kv_scr[slot, 1, h] = kv[:, HEAD_DIM:]

    def projection_stage(slot, rot, tail_is_zero):
        piece = lambda p: functools.partial(in_piece, slot, p, rot, tail_is_zero)
        heads = lambda part: range(part * HEADS_PER_PIECE, (part + 1) * HEADS_PER_PIECE)
        prods = lambda part: (
            [functools.partial(retention_scores, slot, h // 2, c)
             for h in heads(part)[::2] for c in range(CHUNKS_PER_HALF)]
            + [functools.partial(retention_summaries, slot, h) for h in heads(part)])
        order = []
        for part in range(PIECES_PER_GROUP):
            order += [piece(g * PIECES_PER_GROUP + part) for g in (Q_GROUP, K_GROUP, V_GROUP)]
            order += prods(part)
        order += [piece(g * PIECES_PER_GROUP + part) for g in (GRET_GROUP, XLRU_GROUP, GLRU_GROUP)
                  for part in range(PIECES_PER_GROUP)]
        return order

    def retention_unit(slot, hp, c):
        rs = chunk_rows(c)
        h0, h1 = 2 * hp, 2 * hp + 1
        pair = slice(h0 * HEAD_DIM, (h1 + 1) * HEAD_DIM)
        s0, s1 = state_scr[h0], state_scr[h1]
        b0, b1 = s0.astype(BF16), s1.astype(BF16)
        lhs = jnp.concatenate([sd_scr[slot, rs, pair], q_scr[slot, rs, pair]], axis=1)
        rhs = jnp.concatenate([
            block_diagonal(v_scr[slot, rs, head_lanes(h0)], v_scr[slot, rs, head_lanes(h1)]),
            pair_rows(jnp.concatenate([b0[:half_dim], b1[:half_dim]], axis=0),
                      jnp.concatenate([b0[half_dim:], b1[half_dim:]], axis=0))], axis=0)
        o2 = jnp.dot(lhs, rhs, preferred_element_type=F32)
        for h, state, lanes in ((h0, s0, slice(0, HEAD_DIM)), (h1, s1, slice(HEAD_DIM, 2 * HEAD_DIM))):
            hl = head_lanes(h)
            o = o2[:, lanes] * xi_ref[:, hl]
            state_scr[h] = gdec_ref[:, hl] * (state + kv_scr[slot, c, h])
            mu = jnp.mean(o, axis=-1, keepdims=True)
            oc = o - mu
            var = jnp.mean(oc * oc, axis=-1, keepdims=True)
            gate = gret_scr[slot, rs, hl]
            y = oc * lax.rsqrt(var + NORM_EPS) * (gate * _sigmoid(gate))
            mixed_scr[slot, rs, hl] = y.astype(BF16)

    def lru_unit(slot, n, rh):
        sl = slice(n * LRU_BLOCK_DIM, (n + 1) * LRU_BLOCK_DIM)
        rows = slice(rh * LRU_ROWS, (rh + 1) * LRU_ROWS)
        xc = xc_scr[slot, rows, sl]
        pre = jnp.dot(xcb_scr[slot, rows, sl], wg_ref[n], preferred_element_type=F32)
        z = -lam_ref[:, sl]
        softplus = jnp.maximum(z, 0.0) + jnp.log1p(jnp.exp(-jnp.abs(z)))
        r = _sigmoid(pre[:, :LRU_BLOCK_DIM] + ba_ref[:, sl])
        i = _sigmoid(pre[:, LRU_BLOCK_DIM:] + bx_ref[:, sl])
        a = jnp.exp2(r * ((-LRU_C * LOG2_E) * softplus))
        v = 1.0 - a * a
        mult = jnp.where(v > 0.0, v * lax.rsqrt(v), 0.0)
        b = mult * (i * xc)
        hseq, hcarry_scr[:, sl] = _linear_scan_interleaved(a, b, hcarry_scr[:, sl])
        gate = glru_scr[slot, rows, sl]
        y = hseq * (gate * _sigmoid(gate))
        for g in range(LRU_ROWS // GROUP_ROWS):
            for i in range(RUN):
                src = slice(g * GROUP_ROWS + i * V7X_SUBLANES, g * GROUP_ROWS + (i + 1) * V7X_SUBLANES)
                dst = pl.ds(rows.start + g * GROUP_ROWS + i, V7X_SUBLANES, stride=RUN)
                yslab_scr[slot, n, dst, :] = y[src]

    def pack_lru_outputs(half):
        for n in range(LRU_BLOCKS):
            osl = slice(D_MODEL + n * LRU_BLOCK_DIM, D_MODEL + (n + 1) * LRU_BLOCK_DIM)
            mixed_scr[half, :, osl] = yslab_scr[half, n].astype(BF16)

    def mixer_stage(slot):
        lru = [functools.partial(lru_unit, slot, n, rh)
               for n in range(LRU_BLOCKS) for rh in range(HALF_BLOCK // LRU_ROWS)]
        ret = [functools.partial(retention_unit, slot, hp, c)
               for c in range(CHUNKS_PER_HALF) for hp in range(RET_HEADS // 2)]
        return _interleave(lru, ret)

    def out_piece(half, p):
        cols = slice(p * MXU_PIECE_COLS, (p + 1) * MXU_PIECE_COLS)
        resid = x_ref[0, 0:hb, cols] if half == 0 else xstash_scr[:, cols]
        hres_scr[:, cols] = resid + jnp.dot(mixed_scr[half], wout_scr[:, cols],
                                            preferred_element_type=F32)

    def norm_output(half):
        hres = hres_scr[...]
        ms = jnp.mean(hres * hres, axis=-1, keepdims=True)
        y = hres * lax.rsqrt(ms + NORM_EPS) * gout_ref[...]
        if half == 0:
            out_stage[...] = y
        else:
            out_ref[0, hb:2 * hb, :] = y

    def output_stage(half):
        return [functools.partial(pack_lru_outputs, half)] + [
            functools.partial(out_piece, half, p) for p in range(PIECES_PER_GROUP)] + [
            functools.partial(norm_output, half)]

    @pl.when(step < n_steps)
    def _():
        @pl.when(step == 0)
        def _():
            stages = [buf.at[slot] for buf in (gret_scr, xc_scr, glru_scr) for slot in range(2)]
            _load_weights_as_bf16([(win_hbm, win_scr, (Q_GROUP, K_GROUP)), (wout_hbm, wout_scr, ())],
                                  stages, weight_sems)
            norm_input(x_ref[0, 0:hb, :], 0)
            tail_scr[...] = jnp.zeros_like(tail_scr)
            rot = (cos_ref[0:hb, :], sin_ref[0:hb, :], cost_ref[:, 0:hb], sint_ref[:, 0:hb])
            for thunk in projection_stage(0, rot, None):
                thunk()
            mixed_scr[1] = jnp.zeros(mixed_scr.shape[1:], BF16)
            yslab_scr[1] = jnp.zeros(yslab_scr.shape[1:], F32)
            xstash_scr[...] = jnp.zeros_like(xstash_scr)
            out_stage[...] = jnp.zeros_like(out_stage)

        @pl.when(step % steps_per_row == 0)
        def _():
            state_scr[...] = jnp.zeros_like(state_scr)
            hcarry_scr[...] = jnp.zeros_like(hcarry_scr)

        out_ref[0, 0:hb, :] = out_stage[...]
        norm_input(x_ref[0, hb:2 * hb, :], 1)
        rot = (cos_ref[hb:2 * hb, :], sin_ref[hb:2 * hb, :], cost_ref[:, hb:2 * hb], sint_ref[:, hb:2 * hb])
        matrix = output_stage(1) + projection_stage(1, rot, None)
        for thunk in _interleave(mixer_stage(0), matrix):
            thunk()
        xstash_scr[...] = x_ref[0, hb:2 * hb, :]
        norm_input(xnext_ref[0], 0)
        next_starts_row = (step + 1) % steps_per_row == 0
        rot = (cosn_ref[...], sinn_ref[...], cosnt_ref[...], sinnt_ref[...])
        matrix = output_stage(0) + projection_stage(0, rot, next_starts_row)
        for thunk in _interleave(mixer_stage(1), matrix):
            thunk()

    @pl.when(step == n_steps)
    def _():
        out_ref[0, 0:hb, :] = out_stage[...]
        for thunk in output_stage(1):
            thunk()


def _position_tables(seq_len):
    half = np.arange(0, HEAD_DIM, 2, dtype=np.float64)
    inv_freq = ROPE_BASE ** (-half / HEAD_DIM)
    ang = np.arange(seq_len, dtype=np.float64)[:, None] * inv_freq[None, :]
    cos, sin = np.cos(ang), np.sin(ang)
    cos_t = np.concatenate([cos, cos], axis=-1).astype(np.float32)
    sin_t = np.concatenate([sin, sin], axis=-1).astype(np.float32)
    return cos_t, sin_t


def _decay_tables():
    c = RET_CHUNK
    scale = HEAD_DIM ** -0.5
    log_g = np.log1p(-np.exp2(-5.0 - np.arange(RET_HEADS, dtype=np.float64)))
    idx = np.arange(c, dtype=np.float64)
    xi = np.exp((idx + 1)[None, :] * log_g[:, None])
    kscale = scale * np.exp(-(idx + 1)[None, :] * log_g[:, None])
    gdec = np.exp(c * log_g)

    def rows_by_head_lanes(t):
        return np.repeat(t.T[:, :, None], HEAD_DIM, axis=2).reshape(c, RET_HEADS * HEAD_DIM)

    f32 = lambda t: t.astype(np.float32)
    kscale_pairs = np.repeat(kscale.reshape(RET_HEADS // 2, 2, 1, c), HEAD_DIM // 2, axis=2)
    kscale_pairs = kscale_pairs.reshape(RET_HEADS // 2, HEAD_DIM, c)
    return f32(kscale_pairs), f32(rows_by_head_lanes(xi)), f32(np.repeat(gdec, HEAD_DIM)[None, :])


def _resident(arr):
    nd = arr.ndim
    return pl.BlockSpec(arr.shape, lambda i: (0,) * nd, pipeline_mode=pl.Buffered(1))


_IN_HBM = pl.BlockSpec(memory_space=pl.ANY)


@jax.jit
def kernel(x, norm_in_g, w_in, conv_w, conv_b, gate_a_w, gate_a_b, gate_x_w, gate_x_b,
           lru_lambda, w_out, norm_out_g):
    batch, seq_len, d_model = x.shape
    assert d_model == D_MODEL and seq_len % STEP_BLOCK == 0
    assert w_in.shape == (D_MODEL, IN_WIDTH) and w_out.shape == (MIX_WIDTH, D_MODEL)
    steps_per_row = seq_len // STEP_BLOCK
    n_steps = batch * steps_per_row
    halves_per_row = 2 * steps_per_row

    cos_t, sin_t = _position_tables(seq_len)
    kscale_t, xi_t, gdec_t = _decay_tables()
    w_gates = jnp.concatenate([gate_a_w, gate_x_w], axis=-1).astype(BF16)
    row = lambda p: p.reshape(1, D_MODEL).astype(F32)

    def next_half(i):
        n = jnp.minimum(2 * (i + 1), 2 * n_steps - 2)
        return (n // halves_per_row, n % halves_per_row, 0)

    def block_of_step(i):
        j = jnp.minimum(i, n_steps - 1)
        return (j // steps_per_row, j % steps_per_row, 0)

    def block_of_previous_step(i):
        j = jnp.maximum(i - 1, 0)
        return (j // steps_per_row, j % steps_per_row, 0)

    step_rows = pl.BlockSpec((1, STEP_BLOCK, D_MODEL), block_of_step)
    step_table = pl.BlockSpec((STEP_BLOCK, HEAD_DIM), lambda i: block_of_step(i)[1:])
    next_table = pl.BlockSpec((HALF_BLOCK, HEAD_DIM), lambda i: next_half(i)[1:])
    step_table_ft = pl.BlockSpec((HEAD_DIM, STEP_BLOCK), lambda i: (0, block_of_step(i)[1]))
    next_table_ft = pl.BlockSpec((HEAD_DIM, HALF_BLOCK), lambda i: (0, next_half(i)[1]))
    cos_ft, sin_ft = np.ascontiguousarray(cos_t.T), np.ascontiguousarray(sin_t.T)
    small = (row(norm_in_g), conv_w.astype(F32), row(conv_b), w_gates, row(gate_a_b),
             row(gate_x_b), row(lru_lambda), row(norm_out_g), kscale_t, xi_t, gdec_t)
    (gin, convw, convb, wg, ba, bx, lam, gout, kscale_t, xi_t, gdec_t) = small
    operands_and_specs = (
        (x, step_rows), (x, pl.BlockSpec((1, HALF_BLOCK, D_MODEL), next_half)),
        (cos_t, step_table), (sin_t, step_table), (cos_t, next_table), (sin_t, next_table),
        (cos_ft, step_table_ft), (sin_ft, step_table_ft), (cos_ft, next_table_ft), (sin_ft, next_table_ft),
        (gin, _resident(gin)), (w_in.astype(F32), _IN_HBM), (convw, _resident(convw)),
        (convb, _resident(convb)), (wg, _resident(wg)), (ba, _resident(ba)), (bx, _resident(bx)),
        (lam, _resident(lam)), (w_out.astype(F32), _IN_HBM), (gout, _resident(gout)),
        (kscale_t, _resident(kscale_t)), (xi_t, _resident(xi_t)), (gdec_t, _resident(gdec_t)),
    )
    operands = [op for op, _ in operands_and_specs]
    in_specs = [spec for _, spec in operands_and_specs]

    half_f32 = pltpu.VMEM((2, HALF_BLOCK, D_MODEL), F32)
    half_bf16 = pltpu.VMEM((2, HALF_BLOCK, D_MODEL), BF16)
    scratch_shapes = [
        pltpu.VMEM((D_MODEL, IN_WIDTH), BF16),
        pltpu.VMEM((MIX_WIDTH, D_MODEL), BF16),
        pltpu.SemaphoreType.DMA((WEIGHT_STAGES,)),
        half_bf16,
        pltpu.VMEM((D_MODEL, HALF_BLOCK), BF16),
        half_bf16,
        half_bf16,
        pltpu.VMEM((2, CHUNKS_PER_HALF, RET_HEADS, HEAD_DIM, HEAD_DIM), F32),
        half_f32,
        half_f32,
        half_bf16,
        half_f32,
        half_bf16,
        half_bf16,
        pltpu.VMEM((D_MODEL // LRU_BLOCK_DIM, HALF_BLOCK, LRU_BLOCK_DIM), F32),
        pltpu.VMEM((2, LRU_BLOCKS, HALF_BLOCK, LRU_BLOCK_DIM), F32),
        pltpu.VMEM((HALF_BLOCK, D_MODEL), F32),
        pltpu.VMEM((RET_HEADS, HEAD_DIM, HEAD_DIM), F32),
        pltpu.VMEM((1, D_MODEL), F32),
        pltpu.VMEM((CONV_WIDTH - 1, V7X_SUBLANES, D_MODEL), F32),
        pltpu.VMEM((2, HALF_BLOCK, MIX_WIDTH), BF16),
        pltpu.VMEM((HALF_BLOCK, D_MODEL), F32),
        pltpu.VMEM((HALF_BLOCK, D_MODEL), F32),
    ]

    nbytes = lambda shape, dtype: int(np.prod(shape)) * jnp.dtype(dtype).itemsize
    f32_rows = lambda rows, width: nbytes((rows, width), F32)
    windows = 2 * (f32_rows(2 * STEP_BLOCK + HALF_BLOCK, D_MODEL)
                   + 4 * f32_rows(STEP_BLOCK + HALF_BLOCK, HEAD_DIM))
    vmem_bytes = (sum(nbytes(s.shape, s.dtype) for s in scratch_shapes if s.memory_space == pltpu.VMEM)
                  + windows + sum(nbytes(a.shape, a.dtype) for a in small) + COMPILER_SCRATCH_BYTES)
    assert vmem_bytes <= V7X_VMEM_BYTES, vmem_bytes

    return pl.pallas_call(
        functools.partial(_layer_body, steps_per_row=steps_per_row, n_steps=n_steps),
        grid=(n_steps + 1,),
        in_specs=in_specs,
        out_specs=pl.BlockSpec((1, STEP_BLOCK, D_MODEL), block_of_previous_step),
        out_shape=jax.ShapeDtypeStruct(x.shape, x.dtype),
        scratch_shapes=scratch_shapes,
        compiler_params=pltpu.CompilerParams(
            dimension_semantics=("arbitrary",),
            vmem_limit_bytes=vmem_bytes,
        ),
        name="hybrid_layer",
    )(*operands)
```
